```python
import jax, jax.numpy as jnp
from jax import lax
import numpy as np

D_MODEL = 1024
BATCH = 8
SEQ = 8192
DEPTH = 1

CHUNK = 64
D_MIX = D_MODEL
D_RNN = D_MIX // 2
RNN_BLOCKS = 8
RNN_BLOCK = D_RNN // RNN_BLOCKS
CONV_WIDTH = 4
RG_C = 8.0
N_HEADS = 8
HEAD_DIM = 64
D_ATTN = N_HEADS * HEAD_DIM
IDX_HEADS = 8
IDX_DIM = 64
TOPK_KEYS_MAX = 256
Q_BLOCK = 128
N_EXPERTS = 32
TOPK_EXPERTS = 4
D_FF = D_MODEL
SWIGLU_ALPHA = 1.702
SWIGLU_LIMIT = 7.0
MOE_BLOCK = 256
EPS = 1e-6
IN_SPLITS = [D_RNN, D_RNN, D_ATTN, D_ATTN, D_ATTN, IDX_HEADS * IDX_DIM, IDX_DIM, IDX_HEADS]
D_IN = sum(IN_SPLITS)

kernel_name = "hybrid_rglru_dsa_moe_adaln"


def rms_norm(x, g):
    xf = x.astype(jnp.float32)
    y = xf * lax.rsqrt(jnp.mean(xf * xf, axis=-1, keepdims=True) + EPS)
    return (y * g.astype(jnp.float32)).astype(x.dtype)


def rg_lru_branch(xr, xg, conv_w, conv_b, w_a, b_a, w_x, b_x, lam):
    B, S, _ = xr.shape
    xc = lax.conv_general_dilated(
        xr, conv_w[:, None, :].astype(xr.dtype), window_strides=(1,),
        padding=[(CONV_WIDTH - 1, 0)], dimension_numbers=("NWC", "WIO", "NWC"),
        feature_group_count=D_RNN) + conv_b
    xb = xc.reshape(B, S, RNN_BLOCKS, RNN_BLOCK)
    r = jax.nn.sigmoid((jnp.einsum("bshi,hij->bshj", xb, w_a).reshape(B, S, D_RNN) + b_a).astype(jnp.float32))
    i = jax.nn.sigmoid((jnp.einsum("bshi,hij->bshj", xb, w_x).reshape(B, S, D_RNN) + b_x).astype(jnp.float32))
    log_a = -RG_C * r * jax.nn.softplus(-lam.astype(jnp.float32))
    a = jnp.exp(log_a)
    mult = jnp.sqrt(-jnp.expm1(2.0 * log_a))
    u = mult * i * xc.astype(jnp.float32)

    def combine(left, right):
        a1, b1 = left
        a2, b2 = right
        return a1 * a2, a2 * b1 + b2

    _, h = lax.associative_scan(combine, (a, u), axis=1)
    return (jax.nn.gelu(xg.astype(jnp.float32)) * h).astype(xr.dtype)


def dsa_attention(q, k, v, q_idx, k_idx, w_idx):
    B, S = q.shape[0], q.shape[1]
    n_sel = min(TOPK_KEYS_MAX, S // 4)
    n_qb = S // Q_BLOCK
    key_chunk = jnp.arange(S) // CHUNK
    k_idx_f = k_idx.astype(jnp.float32)
    gather = jax.vmap(lambda t, idx: t[idx])

    def to_blocks(t):
        return jnp.swapaxes(t.reshape((B, n_qb, Q_BLOCK) + t.shape[2:]), 0, 1)

    def one_block(args):
        qb, qib, wb, bi = args
        q_chunk = (bi * Q_BLOCK + jnp.arange(Q_BLOCK)) // CHUNK
        s = jnp.einsum("bqhd,bsd->bqhs", qib.astype(jnp.float32), k_idx_f) * (IDX_DIM ** -0.5)
        score = jnp.einsum("bqh,bqhs->bqs", wb.astype(jnp.float32) * (IDX_HEADS ** -0.5), jax.nn.relu(s))
        admissible = key_chunk[None, :] <= q_chunk[:, None]
        score = jnp.where(admissible[None], score, -jnp.inf)
        _, sel = lax.top_k(score, n_sel)
        valid = (sel // CHUNK) <= q_chunk[None, :, None]
        ks = gather(k, sel)
        vs = gather(v, sel)
        logits = jnp.einsum("bqhd,bqkhd->bqhk", qb, ks).astype(jnp.float32) * (HEAD_DIM ** -0.5)
        logits = jnp.where(valid[:, :, None, :], logits, -jnp.inf)
        p = jax.nn.softmax(logits, axis=-1).astype(v.dtype)
        return jnp.einsum("bqhk,bqkhd->bqhd", p, vs)

    out = lax.map(one_block, (to_blocks(q), to_blocks(q_idx), to_blocks(w_idx), jnp.arange(n_qb)))
    return jnp.swapaxes(out, 0, 1).reshape(B, S, D_ATTN)


def moe_ffn(h, w_router, b_router, w1, b1, w2, b2):
    T, D = h.shape
    logits = (h @ w_router + b_router).astype(jnp.float32)
    top_val, top_e = lax.top_k(logits, TOPK_EXPERTS)
    gates = jax.nn.softmax(top_val, axis=-1).astype(h.dtype)
    TK = T * TOPK_EXPERTS
    e_flat = top_e.reshape(TK)
    tok_flat = jnp.arange(TK, dtype=jnp.int32) // TOPK_EXPERTS
    order = jnp.argsort(e_flat, stable=True)
    e_sorted = e_flat[order]
    counts = jnp.bincount(e_flat, length=N_EXPERTS)
    group_start = jnp.cumsum(counts) - counts
    padded = (counts + MOE_BLOCK - 1) // MOE_BLOCK * MOE_BLOCK
    pad_end = jnp.cumsum(padded)
    pad_start = pad_end - padded
    dest = pad_start[e_sorted] + jnp.arange(TK, dtype=jnp.int32) - group_start[e_sorted]
    n_rows = TK + N_EXPERTS * MOE_BLOCK
    n_blocks = n_rows // MOE_BLOCK
    row_tok = jnp.full((n_rows,), T, jnp.int32).at[dest].set(tok_flat[order])
    row_gate = jnp.zeros((n_rows,), h.dtype).at[dest].set(gates.reshape(TK)[order])
    block_e = jnp.minimum(jnp.searchsorted(pad_end, jnp.arange(n_blocks, dtype=jnp.int32) * MOE_BLOCK, side="right"), N_EXPERTS - 1)
    h_pad = jnp.concatenate([h, jnp.zeros((1, D), h.dtype)], axis=0)

    def expert_block(args):
        tok, gate, e = args
        u = h_pad[tok] @ w1[e] + b1[e]
        u_glu = jnp.minimum(u[:, ::2], SWIGLU_LIMIT)
        u_lin = jnp.clip(u[:, 1::2], -SWIGLU_LIMIT, SWIGLU_LIMIT)
        act = u_glu * jax.nn.sigmoid(SWIGLU_ALPHA * u_glu) * (u_lin + 1)
        return (act @ w2[e] + b2[e]) * gate[:, None]

    y = lax.map(expert_block, (row_tok.reshape(n_blocks, MOE_BLOCK), row_gate.reshape(n_blocks, MOE_BLOCK), block_e))
    return jax.ops.segment_sum(y.reshape(n_rows, D), row_tok, num_segments=T + 1)[:T]


def hybrid_layer(x, c, w_ada, b_ada, norm1_g, w_in, conv_w, conv_b, w_rg_a, b_rg_a, w_rg_x, b_rg_x,
                 lru_lambda, q_norm_g, k_norm_g, kidx_norm_g, rg_out_g, attn_out_g, w_out, norm2_g,
                 w_router, b_router, w1, b1, w2, b2):
    B, S, D = x.shape
    mod = jax.nn.silu(c) @ w_ada + b_ada
    sh1, sc1, g1, sh2, sc2, g2 = jnp.split(mod[:, None, :], 6, axis=-1)
    h = rms_norm(x, norm1_g) * (1 + sc1) + sh1
    z = h @ w_in
    xr, xg, q, k, v, qi, ki, wi = jnp.split(z, [int(s) for s in np.cumsum(IN_SPLITS)[:-1]], axis=-1)
    y_rnn = rg_lru_branch(xr, xg, conv_w, conv_b, w_rg_a, b_rg_a, w_rg_x, b_rg_x, lru_lambda)
    q = rms_norm(q.reshape(B, S, N_HEADS, HEAD_DIM), q_norm_g)
    k = rms_norm(k.reshape(B, S, N_HEADS, HEAD_DIM), k_norm_g)
    v = v.reshape(B, S, N_HEADS, HEAD_DIM)
    qi = qi.reshape(B, S, IDX_HEADS, IDX_DIM)
    ki = rms_norm(ki, kidx_norm_g)
    y_attn = dsa_attention(q, k, v, qi, ki, wi)
    mix = jnp.concatenate([rms_norm(y_rnn, rg_out_g), rms_norm(y_attn, attn_out_g)], axis=-1) @ w_out
    x = x + g1 * mix
    h2 = rms_norm(x, norm2_g) * (1 + sc2) + sh2
    ff = moe_ffn(h2.reshape(B * S, D), w_router, b_router, w1, b1, w2, b2).reshape(B, S, D)
    return x + g2 * ff


def setup_inputs(seed: int = 0) -> dict:
    key = jax.random.key(seed)
    ks = jax.random.split(key, 26)
    f32 = jnp.float32
    L = DEPTH

    def nrm(k, shape, scale):
        return jax.random.normal(k, shape, f32) * scale

    a_c = jax.random.uniform(ks[12], (L, D_RNN), f32, 0.81, 0.998)
    sig = a_c ** (1.0 / RG_C)
    return {
        "x": nrm(ks[0], (BATCH, SEQ, D_MODEL), 1.0),
        "c": nrm(ks[1], (BATCH, D_MODEL), 1.0),
        "w_ada": nrm(ks[2], (L, D_MODEL, 6 * D_MODEL), 0.5 * D_MODEL ** -0.5),
        "b_ada": nrm(ks[3], (L, 6 * D_MODEL), 0.02),
        "norm1_g": 1.0 + nrm(ks[4], (L, D_MODEL), 0.02),
        "w_in": nrm(ks[5], (L, D_MODEL, D_IN), D_MODEL ** -0.5),
        "conv_w": nrm(ks[6], (L, CONV_WIDTH, D_RNN), CONV_WIDTH ** -0.5),
        "conv_b": nrm(ks[7], (L, D_RNN), 0.02),
        "w_rg_a": nrm(ks[8], (L, RNN_BLOCKS, RNN_BLOCK, RNN_BLOCK), RNN_BLOCK ** -0.5),
        "b_rg_a": nrm(ks[9], (L, D_RNN), 0.02),
        "w_rg_x": nrm(ks[10], (L, RNN_BLOCKS, RNN_BLOCK, RNN_BLOCK), RNN_BLOCK ** -0.5),
        "b_rg_x": nrm(ks[11], (L, D_RNN), 0.02),
        "lru_lambda": jnp.log(sig) - jnp.log1p(-sig),
        "q_norm_g": 1.0 + nrm(ks[13], (L, HEAD_DIM), 0.02),
        "k_norm_g": 1.0 + nrm(ks[14], (L, HEAD_DIM), 0.02),
        "kidx_norm_g": 1.0 + nrm(ks[15], (L, IDX_DIM), 0.02),
        "rg_out_g": 1.0 + nrm(ks[16], (L, D_RNN), 0.02),
        "attn_out_g": 1.0 + nrm(ks[17], (L, D_ATTN), 0.02),
        "w_out": nrm(ks[18], (L, D_MIX, D_MODEL), D_MIX ** -0.5),
        "norm2_g": 1.0 + nrm(ks[19], (L, D_MODEL), 0.02),
        "w_router": nrm(ks[20], (L, D_MODEL, N_EXPERTS), D_MODEL ** -0.5),
        "b_router": nrm(ks[21], (L, N_EXPERTS), 0.01),
        "w1": nrm(ks[22], (L, N_EXPERTS, D_MODEL, 2 * D_FF), D_MODEL ** -0.5),
        "b1": nrm(ks[23], (L, N_EXPERTS, 2 * D_FF), 0.02),
        "w2": nrm(ks[24], (L, N_EXPERTS, D_FF, D_MODEL), D_FF ** -0.5),
        "b2": nrm(ks[25], (L, N_EXPERTS, D_MODEL), 0.02),
    }


def reference(x, c, w_ada, b_ada, norm1_g, w_in, conv_w, conv_b, w_rg_a, b_rg_a, w_rg_x, b_rg_x,
              lru_lambda, q_norm_g, k_norm_g, kidx_norm_g, rg_out_g, attn_out_g, w_out, norm2_g,
              w_router, b_router, w1, b1, w2, b2):
    for l in range(DEPTH):
        x = hybrid_layer(x, c, w_ada[l], b_ada[l], norm1_g[l], w_in[l], conv_w[l], conv_b[l],
                         w_rg_a[l], b_rg_a[l], w_rg_x[l], b_rg_x[l], lru_lambda[l], q_norm_g[l],
                         k_norm_g[l], kidx_norm_g[l], rg_out_g[l], attn_out_g[l], w_out[l], norm2_g[l],
                         w_router[l], b_router[l], w1[l], b1[l], w2[l], b2[l])
    return x
```

```python
import functools

import jax
import jax.numpy as jnp
from jax import lax
from jax.experimental import pallas as pl
from jax.experimental.pallas import tpu as pltpu

D_MODEL = 1024
CHUNK = 64
CHUNK_SHIFT = 6
D_RNN = 512
RNN_BLOCKS = 8
RNN_BLOCK = 64
CONV_WIDTH = 4
RG_C = 8.0
N_HEADS = 8
HEAD_DIM = 64
D_ATTN = 512
IDX_HEADS = 8
IDX_DIM = 64
TOPK_KEYS_MAX = 256
N_EXPERTS = 32
TOPK_EXPERTS = 4
D_FF = 1024
SWIGLU_ALPHA = 1.702
SWIGLU_LIMIT = 7.0
EPS = 1e-6

LANES = 128
IDX_K = 256
NEG_BIG = -1e30
VMEM_LIMIT = 56 * 1024 * 1024

F32 = jnp.float32
BF16 = jnp.bfloat16


def _dot(a, b):
    return jnp.dot(a, b, preferred_element_type=F32)


def _dot_nt(a, b):
    return lax.dot_general(a, b, (((1,), (1,)), ((), ())), preferred_element_type=F32)


def _split(a):
    hi = a.astype(BF16)
    lo = (a - hi.astype(F32)).astype(BF16)
    return hi, lo


def _dot3(a_hi, a_lo, b_hi, b_lo):
    return _dot(a_hi, b_hi) + (_dot(a_hi, b_lo) + _dot(a_lo, b_hi))


def _rms(x, g):
    ms = jnp.mean(x * x, axis=-1, keepdims=True)
    return x * lax.rsqrt(ms + EPS) * g


def _cparams(*sem):
    return pltpu.CompilerParams(dimension_semantics=sem, vmem_limit_bytes=VMEM_LIMIT)


def _ada_kernel(c_ref, w_ref, b_ref, o_ref):
    c = c_ref[...]
    s = c * jax.nn.sigmoid(c)
    s_hi, s_lo = _split(s)
    w_hi, w_lo = _split(w_ref[...])
    o_ref[...] = _dot3(s_hi, s_lo, w_hi, w_lo) + b_ref[...]


def _ada(c, w_ada, b_ada):
    B, D = c.shape
    N = w_ada.shape[1]
    rows = 16
    c_pad = jnp.zeros((rows, D), F32).at[:B].set(c)
    tn = 1024
    out = pl.pallas_call(
        _ada_kernel,
        grid=(N // tn,),
        in_specs=[pl.BlockSpec((rows, D), lambda j: (0, 0)),
                  pl.BlockSpec((D, tn), lambda j: (0, j)),
                  pl.BlockSpec((1, tn), lambda j: (0, j))],
        out_specs=pl.BlockSpec((rows, tn), lambda j: (0, j)),
        out_shape=jax.ShapeDtypeStruct((rows, N), F32),
        compiler_params=_cparams("arbitrary"),
        name="ada",
    )(c_pad, w_ada, b_ada.reshape(1, N))
    return out[:B]


def _inproj_kernel(x_ref, mod_ref, g1_ref, wmain_ref, widx_hi_ref, widx_lo_ref, bd_ref,
                   qg_ref, kg_ref, kig_ref,
                   xr_ref, xg_ref, q_ref, k_ref, v_ref, qi_ref, kiw_ref):
    x = x_ref[0]
    mod = mod_ref[0]
    h = _rms(x, g1_ref[...]) * (1.0 + mod[1:2]) + mod[0:1]
    h_hi, h_lo = _split(h)
    bd = bd_ref[...]

    def head_norm(z, g):
        sq_hi, sq_lo = _split(z * z)
        ms = _dot(sq_hi, bd) + _dot(sq_lo, bd)
        return z * lax.rsqrt(ms + EPS) * g

    xr_ref[0] = _dot(h_hi, wmain_ref[:, 0:512])
    xg_ref[0] = _dot(h_hi, wmain_ref[:, 512:1024])
    zq = _dot(h_hi, wmain_ref[:, 1024:1536])
    q_ref[0] = (head_norm(zq, qg_ref[...]) * (HEAD_DIM ** -0.5)).astype(BF16)
    zk = _dot(h_hi, wmain_ref[:, 1536:2048])
    k_ref[0] = head_norm(zk, kg_ref[...]).astype(BF16)
    v_ref[0] = _dot(h_hi, wmain_ref[:, 2048:2560]).astype(BF16)

    zi = _dot3(h_hi, h_lo, widx_hi_ref[...], widx_lo_ref[...])
    qi_ref[0] = zi[:, 0:512] * (IDX_DIM ** -0.5)
    kw = zi[:, 512:640]
    lane = lax.broadcasted_iota(jnp.int32, kw.shape, 1)
    is_ki = lane < IDX_DIM
    ms = jnp.sum(jnp.where(is_ki, kw * kw, 0.0), axis=-1, keepdims=True) * (1.0 / IDX_DIM)
    kiw_ref[0] = jnp.where(is_ki, kw * lax.rsqrt(ms + EPS) * kig_ref[...], kw * (IDX_HEADS ** -0.5))


def _inproj(x, mod6, norm1_g, w_in, q_norm_g, k_norm_g, kidx_norm_g, tm):
    B, S, D = x.shape
    wmain = w_in[:, :2560].astype(BF16)
    widx = jnp.concatenate([w_in[:, 2560:3144], jnp.zeros((D, 640 - 584), F32)], axis=1)
    widx_hi = widx.astype(BF16)
    widx_lo = (widx - widx_hi.astype(F32)).astype(BF16)
    head = jnp.arange(D_ATTN) // HEAD_DIM
    bd = jnp.where(head[:, None] == head[None, :], 1.0 / HEAD_DIM, 0.0).astype(BF16)
    qg = jnp.tile(q_norm_g, N_HEADS).reshape(1, D_ATTN)
    kg = jnp.tile(k_norm_g, N_HEADS).reshape(1, D_ATTN)
    kig = jnp.concatenate([kidx_norm_g, jnp.zeros((LANES - IDX_DIM,), F32)]).reshape(1, LANES)
    const = lambda shape: pl.BlockSpec(shape, lambda b, s: (0,) * len(shape))
    row = lambda n: pl.BlockSpec((1, tm, n), lambda b, s: (b, s, 0))
    return pl.pallas_call(
        _inproj_kernel,
        grid=(B, S // tm),
        in_specs=[row(D), pl.BlockSpec((1, 6, D), lambda b, s: (b, 0, 0)), const((1, D)),
                  const((D, 2560)), const((D, 640)), const((D, 640)), const((D_ATTN, D_ATTN)),
                  const((1, D_ATTN)), const((1, D_ATTN)), const((1, LANES))],
        out_specs=[row(512), row(512), row(512), row(512), row(512), row(512), row(LANES)],
        out_shape=[jax.ShapeDtypeStruct((B, S, 512), F32), jax.ShapeDtypeStruct((B, S, 512), F32),
                   jax.ShapeDtypeStruct((B, S, 512), BF16), jax.ShapeDtypeStruct((B, S, 512), BF16),
                   jax.ShapeDtypeStruct((B, S, 512), BF16), jax.ShapeDtypeStruct((B, S, 512), F32),
                   jax.ShapeDtypeStruct((B, S, LANES), F32)],
        compiler_params=_cparams("arbitrary", "arbitrary"),
        name="inproj",
    )(x, mod6, norm1_g.reshape(1, D), wmain, widx_hi, widx_lo, bd, qg, kg, kig)


def _rglru_kernel(xr_ref, xg_ref, cw_ref, cb_ref, wa_hi_ref, wa_lo_ref, wx_hi_ref, wx_lo_ref,
                  ba_ref, bx_ref, lam_ref, g_ref, y_ref, ext_ref, hlast_ref, *, ts):
    @pl.when(pl.program_id(1) == 0)
    def _():
        ext_ref[0:8, :] = jnp.zeros((8, D_RNN), F32)
        hlast_ref[...] = jnp.zeros((1, D_RNN), F32)

    xr = xr_ref[0]
    ext_ref[8:8 + ts, :] = xr
    cw = cw_ref[...]
    xc = cb_ref[...] + cw[0:1] * ext_ref[5:5 + ts, :]
    for j in range(1, CONV_WIDTH):
        xc = xc + cw[j:j + 1] * ext_ref[5 + j:5 + j + ts, :]
    ext_ref[0:8, :] = xr[ts - 8:ts]

    xc_hi, xc_lo = _split(xc)
    r = jax.nn.sigmoid(_dot3(xc_hi, xc_lo, wa_hi_ref[...], wa_lo_ref[...]) + ba_ref[...])
    i = jax.nn.sigmoid(_dot3(xc_hi, xc_lo, wx_hi_ref[...], wx_lo_ref[...]) + bx_ref[...])
    lam = lam_ref[...]
    sp = jnp.maximum(-lam, 0.0) + jnp.log1p(jnp.exp(-jnp.abs(lam)))
    log_a = (-RG_C) * r * sp
    a = jnp.exp(log_a)
    mult = jnp.sqrt(-jnp.tanh(log_a) * (a * a + 1.0))
    u = mult * i * xc

    row = lax.broadcasted_iota(jnp.int32, (ts, D_RNN), 0)
    acc_a, acc_h = a, u
    d = 1
    while d < ts:
        sh_a = pltpu.roll(acc_a, d, 0)
        sh_h = pltpu.roll(acc_h, d, 0)
        ok = row >= d
        acc_h = jnp.where(ok, acc_a * sh_h + acc_h, acc_h)
        acc_a = jnp.where(ok, acc_a * sh_a, acc_a)
        d *= 2
    hseq = acc_h + acc_a * hlast_ref[...]
    hlast_ref[...] = hseq[ts - 1:ts]

    xg = xg_ref[0]
    gelu = 0.5 * xg * (1.0 + jnp.tanh(0.7978845608028654 * (xg + 0.044715 * (xg * xg * xg))))
    y_ref[0] = _rms(gelu * hseq, g_ref[...]).astype(BF16)


def _block_diag(w):
    H, n, _ = w.shape
    eye = jnp.eye(H, dtype=w.dtype)
    return (eye[:, None, :, None] * w[:, :, None, :]).reshape(H * n, H * n)


def _rglru(xr, xg, conv_w, conv_b, w_a, b_a, w_x, b_x, lam, g, ts):
    B, S, _ = xr.shape
    wa_hi, wa_lo = _split(_block_diag(w_a))
    wx_hi, wx_lo = _split(_block_diag(w_x))
    const = lambda shape: pl.BlockSpec(shape, lambda b, s: (0,) * len(shape))
    row = pl.BlockSpec((1, ts, D_RNN), lambda b, s: (b, s, 0))
    vec = lambda a: a.reshape(1, D_RNN)
    return pl.pallas_call(
        functools.partial(_rglru_kernel, ts=ts),
        grid=(B, S // ts),
        in_specs=[row, row, const((CONV_WIDTH, D_RNN)), const((1, D_RNN)),
                  const((D_RNN, D_RNN)), const((D_RNN, D_RNN)), const((D_RNN, D_RNN)), const((D_RNN, D_RNN)),
                  const((1, D_RNN)), const((1, D_RNN)), const((1, D_RNN)), const((1, D_RNN))],
        out_specs=row,
        out_shape=jax.ShapeDtypeStruct((B, S, D_RNN), BF16),
        scratch_shapes=[pltpu.VMEM((ts + 8, D_RNN), F32), pltpu.VMEM((1, D_RNN), F32)],
        compiler_params=_cparams("arbitrary", "arbitrary"),
        name="rglru",
    )(xr, xg, conv_w, vec(conv_b), wa_hi, wa_lo, wx_hi, wx_lo, vec(b_a), vec(b_x), vec(lam), vec(g))


def _dsa_kernel(ki_ref, k_ref, vt_ref, qi_ref, q_ref, w_ref, y_ref,
                sc_ref, m_ref, l_ref, acc_ref, *, qb, tk, n_sel, seq):
    j = pl.program_id(1)
    n_kt = ((j + 1) * qb + tk - 1) // tk
    q_idx = j * qb + lax.broadcasted_iota(jnp.int32, (1, qb), 1)
    q_chunk = jnp.right_shift(q_idx, CHUNK_SHIFT)

    def key_index(kt):
        return kt * tk + lax.broadcasted_iota(jnp.int32, (tk, 1), 0)

    w = w_ref[0]

    def score_tile(kt, carry):
        smin, smax = carry
        off = pl.multiple_of(kt * tk, tk)
        ki = ki_ref[0, pl.ds(off, tk), :]
        sc = jnp.zeros((tk, qb), F32)
        for h in range(IDX_HEADS):
            s = _dot_nt(ki, qi_ref[0, h])
            sc = sc + w[h:h + 1] * jnp.maximum(s, 0.0)
        adm = jnp.right_shift(key_index(kt), CHUNK_SHIFT) <= q_chunk
        sc_ref[pl.ds(off, tk), :] = jnp.where(adm, sc, -jnp.inf)
        smin = jnp.minimum(smin, jnp.min(jnp.where(adm, sc, jnp.inf), axis=0, keepdims=True))
        smax = jnp.maximum(smax, jnp.max(jnp.where(adm, sc, -jnp.inf), axis=0, keepdims=True))
        return smin, smax

    smin, smax = lax.fori_loop(0, n_kt, score_tile,
                               (jnp.full((1, qb), jnp.inf, F32), jnp.full((1, qb), -jnp.inf, F32)))

    def count_ge(t):
        def body(kt, c):
            off = pl.multiple_of(kt * tk, tk)
            blk = sc_ref[pl.ds(off, tk), :]
            return c + jnp.sum(jnp.where(blk >= t, 1.0, 0.0), axis=0, keepdims=True)
        return lax.fori_loop(0, n_kt, body, jnp.zeros((1, qb), F32))

    n_adm = (q_chunk + 1) * CHUNK
    take_all = n_adm <= n_sel
    nf = float(n_sel)

    def bis_cond(st):
        return jnp.min(st[5]) < 1.0

    def bis_body(st):
        lo, hi, cnt_hi, thr, exact, done_f, first = st
        done = done_f > 0.0
        mid = jnp.where(first > 0, hi, lo + (hi - lo) * 0.5)
        c = count_ge(mid)
        conv = jnp.logical_and(first == 0, jnp.logical_or(mid <= lo, mid >= hi))
        hit = jnp.logical_and(c == nf, jnp.logical_not(conv))
        ge = c >= nf
        new_done = jnp.logical_or(done, jnp.logical_or(hit, conv))
        thr_n = jnp.where(done, thr, jnp.where(hit, mid, jnp.where(conv, lo, thr)))
        exact_n = jnp.where(done, exact, jnp.where(hit, 1.0, exact))
        upd = jnp.logical_not(new_done)
        top_tie = jnp.logical_and(first > 0, ge)
        lo_n = jnp.where(jnp.logical_and(upd, ge), mid, lo)
        hi_n = jnp.where(jnp.logical_and(upd, jnp.logical_not(ge)), mid, hi)
        cnt_hi_n = jnp.where(jnp.logical_and(upd, jnp.logical_not(ge)), c, cnt_hi)
        cnt_hi_n = jnp.where(top_tie, 0.0, cnt_hi_n)
        return lo_n, hi_n, cnt_hi_n, thr_n, exact_n, jnp.where(new_done, 1.0, 0.0), jnp.zeros_like(first)

    zeros = jnp.zeros((1, qb), F32)
    take_all_f = jnp.where(take_all, 1.0, 0.0)
    st0 = (smin, smax, zeros, jnp.full((1, qb), -jnp.inf, F32), take_all_f, take_all_f,
           jnp.ones((1, qb), F32))
    lo, hi, cnt_hi, thr, exact, _, _ = lax.while_loop(bis_cond, bis_body, st0)
    need = nf - cnt_hi

    def count_tie(jcut):
        def body(kt, c):
            off = pl.multiple_of(kt * tk, tk)
            blk = sc_ref[pl.ds(off, tk), :]
            hitm = jnp.logical_and(blk == thr, key_index(kt) <= jcut)
            return c + jnp.sum(jnp.where(hitm, 1.0, 0.0), axis=0, keepdims=True)
        return lax.fori_loop(0, n_kt, body, jnp.zeros((1, qb), F32))

    def tie_cond(st):
        jlo, jhi = st
        return jnp.max(jnp.where(jnp.logical_and(exact == 0.0, jlo < jhi), 1.0, 0.0)) > 0.0

    def tie_body(st):
        jlo, jhi = st
        jmid = jnp.right_shift(jlo + jhi, 1)
        ok = count_tie(jmid) >= need
        return jnp.where(ok, jlo, jmid + 1), jnp.where(ok, jmid, jhi)

    jlo, _ = lax.while_loop(tie_cond, tie_body,
                            (jnp.zeros((1, qb), jnp.int32), jnp.full((1, qb), seq - 1, jnp.int32)))
    jcut = jnp.where(take_all, -1, jnp.where(exact > 0.0, seq, jlo))

    m_ref[...] = jnp.full((N_HEADS, qb), NEG_BIG, F32)
    l_ref[...] = jnp.zeros((N_HEADS, qb), F32)
    acc_ref[...] = jnp.zeros((D_ATTN, qb), F32)
    lane = lax.broadcasted_iota(jnp.int32, (qb, LANES), 1)
    q_heads = []
    for p in range(N_HEADS // 2):
        qp = q_ref[0, :, p * LANES:(p + 1) * LANES]
        q_heads.append(jnp.where(lane < HEAD_DIM, qp, jnp.zeros_like(qp)))
        q_heads.append(jnp.where(lane >= HEAD_DIM, qp, jnp.zeros_like(qp)))

    def attn_tile(kt, carry):
        off = pl.multiple_of(kt * tk, tk)
        blk = sc_ref[pl.ds(off, tk), :]
        sel = jnp.logical_or(blk > thr, jnp.logical_and(blk == thr, key_index(kt) <= jcut))
        bias = jnp.where(sel, 0.0, NEG_BIG)
        for h in range(N_HEADS):
            p = h // 2
            kp = k_ref[0, pl.ds(off, tk), p * LANES:(p + 1) * LANES]
            logits = _dot_nt(kp, q_heads[h]) + bias
            m_old = m_ref[h:h + 1, :]
            m_new = jnp.maximum(m_old, jnp.max(logits, axis=0, keepdims=True))
            alpha = jnp.exp(m_old - m_new)
            pr = jnp.exp(logits - m_new)
            l_ref[h:h + 1, :] = alpha * l_ref[h:h + 1, :] + jnp.sum(pr, axis=0, keepdims=True)
            m_ref[h:h + 1, :] = m_new
            vt = vt_ref[0, h * HEAD_DIM:(h + 1) * HEAD_DIM, pl.ds(off, tk)]
            pv = _dot(vt, pr.astype(BF16))
            rows = slice(h * HEAD_DIM, (h + 1) * HEAD_DIM)
            acc_ref[rows, :] = alpha * acc_ref[rows, :] + pv
        return carry

    lax.fori_loop(0, n_kt, attn_tile, 0)
    for h in range(N_HEADS):
        rows = slice(h * HEAD_DIM, (h + 1) * HEAD_DIM)
        y_ref[0, rows, :] = acc_ref[rows, :] / l_ref[h:h + 1, :]


def _dsa(ki_ext, k, vt, qi_ext, q, w_t, qb, tk):
    B, S, _ = k.shape
    n_sel = min(TOPK_KEYS_MAX, S // 4)
    resident = lambda shape: pl.BlockSpec(shape, lambda b, j: (b, 0, 0), pipeline_mode=pl.Buffered(1))
    return pl.pallas_call(
        functools.partial(_dsa_kernel, qb=qb, tk=tk, n_sel=n_sel, seq=S),
        grid=(B, S // qb),
        in_specs=[resident((1, S, IDX_K)), resident((1, S, D_ATTN)), resident((1, D_ATTN, S)),
                  pl.BlockSpec((1, IDX_HEADS, qb, IDX_K), lambda b, j: (b, 0, j, 0)),
                  pl.BlockSpec((1, qb, D_ATTN), lambda b, j: (b, j, 0)),
                  pl.BlockSpec((1, IDX_HEADS, qb), lambda b, j: (b, 0, j))],
        out_specs=pl.BlockSpec((1, D_ATTN, qb), lambda b, j: (b, 0, j)),
        out_shape=jax.ShapeDtypeStruct((B, D_ATTN, S), F32),
        scratch_shapes=[pltpu.VMEM((S, qb), F32), pltpu.VMEM((N_HEADS, qb), F32),
                        pltpu.VMEM((N_HEADS, qb), F32), pltpu.VMEM((D_ATTN, qb), F32)],
        compiler_params=_cparams("arbitrary", "arbitrary"),
        name="dsa",
    )(ki_ext, k, vt, qi_ext, q, w_t)


def _outproj_kernel(x_ref, yr_ref, ya_ref, mod_ref, ag_ref, wor_ref, woa_ref, n2g_ref,
                    wr_hi_ref, wr_lo_ref, br_ref,
                    x1_ref, h2_ref, meta_ref, cnt_ref, base_ref, *, tm):
    @pl.when(jnp.logical_and(pl.program_id(0) == 0, pl.program_id(1) == 0))
    def _():
        base_ref[...] = jnp.zeros((1, LANES), F32)

    mod = mod_ref[0]
    ya = _rms(ya_ref[0], ag_ref[...]).astype(BF16)
    mix = _dot(yr_ref[0], wor_ref[...]) + _dot(ya, woa_ref[...])
    x1 = x_ref[0] + mod[2:3] * mix
    x1_ref[0] = x1
    h2 = _rms(x1, n2g_ref[...]) * (1.0 + mod[4:5]) + mod[3:4]
    h2_ref[0] = h2.astype(BF16)
    h2_hi, h2_lo = _split(h2)
    logits = _dot3(h2_hi, h2_lo, wr_hi_ref[...], wr_lo_ref[...]) + br_ref[...]

    lane = lax.broadcasted_iota(jnp.int32, (tm, LANES), 1).astype(F32)
    work = logits
    vals, onehots, idxs = [], [], []
    for _ in range(TOPK_EXPERTS):
        mx = jnp.max(work, axis=-1, keepdims=True)
        idx = jnp.min(jnp.where(work == mx, lane, float(LANES)), axis=-1, keepdims=True)
        oh = lane == idx
        vals.append(mx)
        idxs.append(idx)
        onehots.append(oh)
        work = jnp.where(oh, -jnp.inf, work)
    es = [jnp.exp(v - vals[0]) for v in vals]
    denom = es[0] + es[1] + es[2] + es[3]
    assigned = jnp.zeros((tm, LANES), F32)
    for oh in onehots:
        assigned = assigned + jnp.where(oh, 1.0, 0.0)
    r_i = lax.broadcasted_iota(jnp.int32, (tm, tm), 0)
    c_i = lax.broadcasted_iota(jnp.int32, (tm, tm), 1)
    ltri = jnp.where(c_i < r_i, 1.0, 0.0).astype(BF16)
    prior = _dot(ltri, assigned.astype(BF16)) + base_ref[...]
    meta = jnp.zeros((tm, LANES), F32)
    for jj in range(TOPK_EXPERTS):
        rank = jnp.sum(jnp.where(onehots[jj], prior, 0.0), axis=-1, keepdims=True)
        meta = jnp.where(lane == jj, idxs[jj], meta)
        meta = jnp.where(lane == TOPK_EXPERTS + jj, es[jj] / denom, meta)
        meta = jnp.where(lane == 2 * TOPK_EXPERTS + jj, rank, meta)
    meta_ref[0] = meta
    base = base_ref[...] + jnp.sum(assigned, axis=0, keepdims=True)
    base_ref[...] = base
    cnt_ref[...] = base


def _outproj(x, y_rnn, y_attn, mod6, attn_out_g, w_out, norm2_g, w_router, b_router, tm):
    B, S, D = x.shape
    wor = w_out[:D_RNN].astype(BF16)
    woa = w_out[D_RNN:].astype(BF16)
    wr = jnp.zeros((D, LANES), F32).at[:, :N_EXPERTS].set(w_router)
    wr_hi, wr_lo = _split(wr)
    br = jnp.full((1, LANES), NEG_BIG, F32).at[0, :N_EXPERTS].set(b_router)
    const = lambda shape: pl.BlockSpec(shape, lambda b, s: (0,) * len(shape))
    row = lambda n: pl.BlockSpec((1, tm, n), lambda b, s: (b, s, 0))
    return pl.pallas_call(
        functools.partial(_outproj_kernel, tm=tm),
        grid=(B, S // tm),
        in_specs=[row(D), row(D_RNN), row(D_ATTN), pl.BlockSpec((1, 6, D), lambda b, s: (b, 0, 0)),
                  const((1, D_ATTN)), const((D_RNN, D)), const((D_ATTN, D)), const((1, D)),
                  const((D, LANES)), const((D, LANES)), const((1, LANES))],
        out_specs=[row(D), row(D), row(LANES), const((1, LANES))],
        out_shape=[jax.ShapeDtypeStruct((B, S, D), F32), jax.ShapeDtypeStruct((B, S, D), BF16),
                   jax.ShapeDtypeStruct((B, S, LANES), F32), jax.ShapeDtypeStruct((1, LANES), F32)],
        scratch_shapes=[pltpu.VMEM((1, LANES), F32)],
        compiler_params=_cparams("arbitrary", "arbitrary"),
        name="outproj",
    )(x, y_rnn, y_attn, mod6, attn_out_g.reshape(1, D_ATTN), wor, woa, norm2_g.reshape(1, D),
      wr_hi, wr_lo, br)


def _expert_kernel(be_ref, x_ref, w1g_ref, w1l_ref, b1g_ref, b1l_ref, w2_ref, b2_ref, y_ref):
    del be_ref
    x = x_ref[...]
    ug = jnp.minimum(_dot(x, w1g_ref[0]) + b1g_ref[0], SWIGLU_LIMIT)
    ul = jnp.clip(_dot(x, w1l_ref[0]) + b1l_ref[0], -SWIGLU_LIMIT, SWIGLU_LIMIT)
    act = ug * jax.nn.sigmoid(SWIGLU_ALPHA * ug) * (ul + 1.0)
    y_ref[...] = _dot(act.astype(BF16), w2_ref[0]) + b2_ref[0]


def _experts(x_sorted, block_e, w1, b1, w2, b2, mb):
    n_rows, D = x_sorted.shape
    E = w1.shape[0]
    w1g = w1[:, :, 0::2].astype(BF16)
    w1l = w1[:, :, 1::2].astype(BF16)
    b1g = b1[:, 0::2].reshape(E, 1, D_FF)
    b1l = b1[:, 1::2].reshape(E, 1, D_FF)
    wspec = lambda k, n: pl.BlockSpec((1, k, n), lambda i, be: (be[i], 0, 0))
    grid_spec = pltpu.PrefetchScalarGridSpec(
        num_scalar_prefetch=1,
        grid=(n_rows // mb,),
        in_specs=[pl.BlockSpec((mb, D), lambda i, be: (i, 0)),
                  wspec(D, D_FF), wspec(D, D_FF), wspec(1, D_FF), wspec(1, D_FF),
                  wspec(D_FF, D), wspec(1, D)],
        out_specs=pl.BlockSpec((mb, D), lambda i, be: (i, 0)),
    )
    return pl.pallas_call(
        _expert_kernel,
        grid_spec=grid_spec,
        out_shape=jax.ShapeDtypeStruct((n_rows, D), F32),
        compiler_params=_cparams("arbitrary"),
        name="experts",
    )(block_e, x_sorted, w1g, w1l, b1g, b1l, w2.astype(BF16), b2.reshape(E, 1, D))


def _combine_kernel(x1_ref, yg_ref, meta_ref, mod_ref, o_ref):
    meta = meta_ref[0]
    ff = meta[:, TOPK_EXPERTS:TOPK_EXPERTS + 1] * yg_ref[0]
    for jj in range(1, TOPK_EXPERTS):
        ff = ff + meta[:, TOPK_EXPERTS + jj:TOPK_EXPERTS + jj + 1] * yg_ref[jj]
    o_ref[0] = x1_ref[0] + mod_ref[0][5:6] * ff


def _combine(x1, yg, meta, mod6, tm):
    B, S, D = x1.shape
    nt = S // tm
    return pl.pallas_call(
        _combine_kernel,
        grid=(B, nt),
        in_specs=[pl.BlockSpec((1, tm, D), lambda b, s: (b, s, 0)),
                  pl.BlockSpec((TOPK_EXPERTS, tm, D), lambda b, s: (0, b * nt + s, 0)),
                  pl.BlockSpec((1, tm, LANES), lambda b, s: (b, s, 0)),
                  pl.BlockSpec((1, 6, D), lambda b, s: (b, 0, 0))],
        out_specs=pl.BlockSpec((1, tm, D), lambda b, s: (b, s, 0)),
        out_shape=jax.ShapeDtypeStruct((B, S, D), F32),
        compiler_params=_cparams("arbitrary", "arbitrary"),
        name="combine",
    )(x1, yg, meta, mod6)


def _tile(n, pref):
    t = pref
    while n % t:
        t //= 2
    return t


def _layer(x, c, w_ada, b_ada, norm1_g, w_in, conv_w, conv_b, w_rg_a, b_rg_a, w_rg_x, b_rg_x,
           lru_lambda, q_norm_g, k_norm_g, kidx_norm_g, rg_out_g, attn_out_g, w_out, norm2_g,
           w_router, b_router, w1, b1, w2, b2):
    B, S, D = x.shape
    T = B * S
    tm = _tile(S, 512)
    mod6 = _ada(c, w_ada, b_ada).reshape(B, 6, D)

    xr, xg, q, k, v, qi, kiw = _inproj(x, mod6, norm1_g, w_in, q_norm_g, k_norm_g, kidx_norm_g, tm)
    y_rnn = _rglru(xr, xg, conv_w, conv_b, w_rg_a, b_rg_a, w_rg_x, b_rg_x, lru_lambda, rg_out_g,
                   _tile(S, 256))

    ki = kiw[..., :IDX_DIM]
    ki_hi, ki_lo = _split(ki)
    ki_ext = jnp.concatenate([ki_hi, ki_hi, ki_lo, jnp.zeros_like(ki_hi)], axis=-1)
    qi_h = jnp.swapaxes(qi.reshape(B, S, IDX_HEADS, IDX_DIM), 1, 2)
    qi_hi, qi_lo = _split(qi_h)
    qi_ext = jnp.concatenate([qi_hi, qi_lo, qi_hi, jnp.zeros_like(qi_hi)], axis=-1)
    w_t = jnp.swapaxes(kiw[..., IDX_DIM:IDX_DIM + IDX_HEADS], 1, 2)
    vt = jnp.swapaxes(v, 1, 2)
    qb = _tile(S, 256)
    y_attn_t = _dsa(ki_ext, k, vt, qi_ext, q, w_t, qb, qb)
    y_attn = jnp.swapaxes(y_attn_t, 1, 2)

    x1, h2, meta, cnt = _outproj(x, y_rnn, y_attn, mod6, attn_out_g, w_out, norm2_g,
                                 w_router, b_router, tm)

    mb = 256
    top_e = meta[..., 0:TOPK_EXPERTS].astype(jnp.int32).reshape(T, TOPK_EXPERTS)
    rank = meta[..., 2 * TOPK_EXPERTS:3 * TOPK_EXPERTS].astype(jnp.int32).reshape(T, TOPK_EXPERTS)
    counts = cnt[0, :N_EXPERTS].astype(jnp.int32)
    padded = (counts + mb - 1) // mb * mb
    pad_end = jnp.cumsum(padded)
    pad_start = pad_end - padded
    dest = pad_start[top_e] + rank
    n_rows = T * TOPK_EXPERTS + N_EXPERTS * mb
    n_blocks = n_rows // mb
    block_e = jnp.minimum(
        jnp.searchsorted(pad_end, jnp.arange(n_blocks, dtype=jnp.int32) * mb, side="right"),
        N_EXPERTS - 1).astype(jnp.int32)
    tok = jnp.broadcast_to(jnp.arange(T, dtype=jnp.int32)[:, None], (T, TOPK_EXPERTS))
    row_tok = jnp.zeros((n_rows,), jnp.int32).at[dest.reshape(-1)].set(tok.reshape(-1))
    x_sorted = h2.reshape(T, D)[row_tok]
    y_sorted = _experts(x_sorted, block_e, w1, b1, w2, b2, mb)
    yg = y_sorted[dest.T]
    return _combine(x1, yg, meta, mod6, tm)


def kernel(x, c, w_ada, b_ada, norm1_g, w_in, conv_w, conv_b, w_rg_a, b_rg_a, w_rg_x, b_rg_x,
           lru_lambda, q_norm_g, k_norm_g, kidx_norm_g, rg_out_g, attn_out_g, w_out, norm2_g,
           w_router, b_router, w1, b1, w2, b2):
    depth = w_ada.shape[0]
    for l in range(depth):
        x = _layer(x, c, w_ada[l], b_ada[l], norm1_g[l], w_in[l], conv_w[l], conv_b[l],
                   w_rg_a[l], b_rg_a[l], w_rg_x[l], b_rg_x[l], lru_lambda[l], q_norm_g[l],
                   k_norm_g[l], kidx_norm_g[l], rg_out_g[l], attn_out_g[l], w_out[l], norm2_g[l],
                   w_router[l], b_router[l], w1[l], b1[l], w2[l], b2[l])
    return x
```

```python
import functools

import jax
import jax.numpy as jnp
from jax import lax
from jax.experimental import pallas as pl
from jax.experimental.pallas import tpu as pltpu

D_MODEL = 1024
CHUNK = 64
CHUNK_SHIFT = 6
D_RNN = 512
RNN_BLOCKS = 8
RNN_BLOCK = 64
CONV_WIDTH = 4
RG_C = 8.0
N_HEADS = 8
HEAD_DIM = 64
D_ATTN = 512
IDX_HEADS = 8
IDX_DIM = 64
TOPK_KEYS_MAX = 256
N_EXPERTS = 32
TOPK_EXPERTS = 4
D_FF = 1024
SWIGLU_ALPHA = 1.702
SWIGLU_LIMIT = 7.0
EPS = 1e-6

LANES = 128
SUBLANES = 8
TINY_F32 = 1.1754943508222875e-38
REDUCE_CHAINS = 8
IDX_K = 256
NEG_BIG = -1e30
VMEM_LIMIT = 56 * 1024 * 1024

F32 = jnp.float32
BF16 = jnp.bfloat16


def _dot(a, b):
    return jnp.dot(a, b, preferred_element_type=F32)


def _dot_nt(a, b):
    return lax.dot_general(a, b, (((1,), (1,)), ((), ())), preferred_element_type=F32)


def _split(a):
    hi = a.astype(BF16)
    lo = (a - hi.astype(F32)).astype(BF16)
    return hi, lo


def _dot3(a_hi, a_lo, b_hi, b_lo):
    return _dot(a_hi, b_hi) + (_dot(a_hi, b_lo) + _dot(a_lo, b_hi))


def _rms(x, g):
    ms = jnp.mean(x * x, axis=-1, keepdims=True)
    return x * lax.rsqrt(ms + EPS) * g


def _cparams(*sem):
    return pltpu.CompilerParams(dimension_semantics=sem, vmem_limit_bytes=VMEM_LIMIT)


def _ada_kernel(c_ref, w_ref, b_ref, o_ref):
    c = c_ref[...]
    s = c * jax.nn.sigmoid(c)
    s_hi, s_lo = _split(s)
    w_hi, w_lo = _split(w_ref[...])
    o_ref[...] = _dot3(s_hi, s_lo, w_hi, w_lo) + b_ref[...]


def _ada(c, w_ada, b_ada):
    B, D = c.shape
    N = w_ada.shape[1]
    rows = 16
    c_pad = jnp.zeros((rows, D), F32).at[:B].set(c)
    tn = 1024
    out = pl.pallas_call(
        _ada_kernel,
        grid=(N // tn,),
        in_specs=[pl.BlockSpec((rows, D), lambda j: (0, 0)),
                  pl.BlockSpec((D, tn), lambda j: (0, j)),
                  pl.BlockSpec((1, tn), lambda j: (0, j))],
        out_specs=pl.BlockSpec((rows, tn), lambda j: (0, j)),
        out_shape=jax.ShapeDtypeStruct((rows, N), F32),
        compiler_params=_cparams("arbitrary"),
        name="ada",
    )(c_pad, w_ada, b_ada.reshape(1, N))
    return out[:B]


def _inproj_kernel(x_ref, mod_ref, g1_ref, wmain_ref, widx_hi_ref, widx_lo_ref, bd_ref,
                   qg_ref, kg_ref, kig_ref,
                   xr_ref, xg_ref, q_ref, k_ref, v_ref, qi_ref, kiw_ref):
    x = x_ref[0]
    mod = mod_ref[0]
    h = _rms(x, g1_ref[...]) * (1.0 + mod[1:2]) + mod[0:1]
    h_hi, h_lo = _split(h)
    bd = bd_ref[...]

    def head_norm(z, g):
        sq_hi, sq_lo = _split(z * z)
        ms = _dot(sq_hi, bd) + _dot(sq_lo, bd)
        return z * lax.rsqrt(ms + EPS) * g

    xr_ref[0] = _dot(h_hi, wmain_ref[:, 0:512])
    xg_ref[0] = _dot(h_hi, wmain_ref[:, 512:1024])
    zq = _dot(h_hi, wmain_ref[:, 1024:1536])
    q_ref[0] = (head_norm(zq, qg_ref[...]) * (HEAD_DIM ** -0.5)).astype(BF16)
    zk = _dot(h_hi, wmain_ref[:, 1536:2048])
    k_ref[0] = head_norm(zk, kg_ref[...]).astype(BF16)
    v_ref[0] = _dot(h_hi, wmain_ref[:, 2048:2560]).astype(BF16)

    zi = _dot3(h_hi, h_lo, widx_hi_ref[...], widx_lo_ref[...])
    qi_ref[0] = zi[:, 0:512] * (IDX_DIM ** -0.5)
    kw = zi[:, 512:640]
    lane = lax.broadcasted_iota(jnp.int32, kw.shape, 1)
    is_ki = lane < IDX_DIM
    ms = jnp.sum(jnp.where(is_ki, kw * kw, 0.0), axis=-1, keepdims=True) * (1.0 / IDX_DIM)
    kiw_ref[0] = jnp.where(is_ki, kw * lax.rsqrt(ms + EPS) * kig_ref[...], kw * (IDX_HEADS ** -0.5))


def _inproj(x, mod6, norm1_g, w_in, q_norm_g, k_norm_g, kidx_norm_g, tm):
    B, S, D = x.shape
    wmain = w_in[:, :2560].astype(BF16)
    widx = jnp.concatenate([w_in[:, 2560:3144], jnp.zeros((D, 640 - 584), F32)], axis=1)
    widx_hi = widx.astype(BF16)
    widx_lo = (widx - widx_hi.astype(F32)).astype(BF16)
    head = jnp.arange(D_ATTN) // HEAD_DIM
    bd = jnp.where(head[:, None] == head[None, :], 1.0 / HEAD_DIM, 0.0).astype(BF16)
    qg = jnp.tile(q_norm_g, N_HEADS).reshape(1, D_ATTN)
    kg = jnp.tile(k_norm_g, N_HEADS).reshape(1, D_ATTN)
    kig = jnp.concatenate([kidx_norm_g, jnp.zeros((LANES - IDX_DIM,), F32)]).reshape(1, LANES)
    const = lambda shape: pl.BlockSpec(shape, lambda b, s: (0,) * len(shape))
    row = lambda n: pl.BlockSpec((1, tm, n), lambda b, s: (b, s, 0))
    return pl.pallas_call(
        _inproj_kernel,
        grid=(B, S // tm),
        in_specs=[row(D), pl.BlockSpec((1, 6, D), lambda b, s: (b, 0, 0)), const((1, D)),
                  const((D, 2560)), const((D, 640)), const((D, 640)), const((D_ATTN, D_ATTN)),
                  const((1, D_ATTN)), const((1, D_ATTN)), const((1, LANES))],
        out_specs=[row(512), row(512), row(512), row(512), row(512), row(512), row(LANES)],
        out_shape=[jax.ShapeDtypeStruct((B, S, 512), F32), jax.ShapeDtypeStruct((B, S, 512), F32),
                   jax.ShapeDtypeStruct((B, S, 512), BF16), jax.ShapeDtypeStruct((B, S, 512), BF16),
                   jax.ShapeDtypeStruct((B, S, 512), BF16), jax.ShapeDtypeStruct((B, S, 512), F32),
                   jax.ShapeDtypeStruct((B, S, LANES), F32)],
        compiler_params=_cparams("arbitrary", "arbitrary"),
        name="inproj",
    )(x, mod6, norm1_g.reshape(1, D), wmain, widx_hi, widx_lo, bd, qg, kg, kig)


def _rglru_kernel(xr_ref, xg_ref, cw_ref, cb_ref, wa_hi_ref, wa_lo_ref, wx_hi_ref, wx_lo_ref,
                  ba_ref, bx_ref, lam_ref, g_ref, y_ref, ext_ref, hlast_ref, *, ts):
    @pl.when(pl.program_id(1) == 0)
    def _():
        ext_ref[0:8, :] = jnp.zeros((8, D_RNN), F32)
        hlast_ref[...] = jnp.zeros((1, D_RNN), F32)

    xr = xr_ref[0]
    ext_ref[8:8 + ts, :] = xr
    cw = cw_ref[...]
    xc = cb_ref[...] + cw[0:1] * ext_ref[5:5 + ts, :]
    for j in range(1, CONV_WIDTH):
        xc = xc + cw[j:j + 1] * ext_ref[5 + j:5 + j + ts, :]
    ext_ref[0:8, :] = xr[ts - 8:ts]

    xc_hi, xc_lo = _split(xc)
    r = jax.nn.sigmoid(_dot3(xc_hi, xc_lo, wa_hi_ref[...], wa_lo_ref[...]) + ba_ref[...])
    i = jax.nn.sigmoid(_dot3(xc_hi, xc_lo, wx_hi_ref[...], wx_lo_ref[...]) + bx_ref[...])
    lam = lam_ref[...]
    sp = jnp.maximum(-lam, 0.0) + jnp.log1p(jnp.exp(-jnp.abs(lam)))
    log_a = (-RG_C) * r * sp
    a = jnp.exp(log_a)
    mult = jnp.sqrt(-jnp.tanh(log_a) * (a * a + 1.0))
    u = mult * i * xc

    row = lax.broadcasted_iota(jnp.int32, (ts, D_RNN), 0)
    acc_a, acc_h = a, u
    d = 1
    while d < ts:
        sh_a = pltpu.roll(acc_a, d, 0)
        sh_h = pltpu.roll(acc_h, d, 0)
        ok = row >= d
        acc_h = jnp.where(ok, acc_a * sh_h + acc_h, acc_h)
        acc_a = jnp.where(ok, acc_a * sh_a, acc_a)
        d *= 2
    hseq = acc_h + acc_a * hlast_ref[...]
    hlast_ref[...] = hseq[ts - 1:ts]

    xg = xg_ref[0]
    gelu = 0.5 * xg * (1.0 + jnp.tanh(0.7978845608028654 * (xg + 0.044715 * (xg * xg * xg))))
    y_ref[0] = _rms(gelu * hseq, g_ref[...]).astype(BF16)


def _block_diag(w):
    H, n, _ = w.shape
    eye = jnp.eye(H, dtype=w.dtype)
    return (eye[:, None, :, None] * w[:, :, None, :]).reshape(H * n, H * n)


def _rglru(xr, xg, conv_w, conv_b, w_a, b_a, w_x, b_x, lam, g, ts):
    B, S, _ = xr.shape
    wa_hi, wa_lo = _split(_block_diag(w_a))
    wx_hi, wx_lo = _split(_block_diag(w_x))
    const = lambda shape: pl.BlockSpec(shape, lambda b, s: (0,) * len(shape))
    row = pl.BlockSpec((1, ts, D_RNN), lambda b, s: (b, s, 0))
    vec = lambda a: a.reshape(1, D_RNN)
    return pl.pallas_call(
        functools.partial(_rglru_kernel, ts=ts),
        grid=(B, S // ts),
        in_specs=[row, row, const((CONV_WIDTH, D_RNN)), const((1, D_RNN)),
                  const((D_RNN, D_RNN)), const((D_RNN, D_RNN)), const((D_RNN, D_RNN)), const((D_RNN, D_RNN)),
                  const((1, D_RNN)), const((1, D_RNN)), const((1, D_RNN)), const((1, D_RNN))],
        out_specs=row,
        out_shape=jax.ShapeDtypeStruct((B, S, D_RNN), BF16),
        scratch_shapes=[pltpu.VMEM((ts + 8, D_RNN), F32), pltpu.VMEM((1, D_RNN), F32)],
        compiler_params=_cparams("arbitrary", "arbitrary"),
        name="rglru",
    )(xr, xg, conv_w, vec(conv_b), wa_hi, wa_lo, wx_hi, wx_lo, vec(b_a), vec(b_x), vec(lam), vec(g))


def _dsa_kernel(ki_ref, k_ref, vt_ref, qi_ref, q_ref, w_ref, y_ref,
                sc_ref, qh_ref, acc_ref, *, qb, tk, n_sel, seq):
    j = pl.program_id(1)
    n_kt = ((j + 1) * qb + tk - 1) // tk
    q_idx = j * qb + lax.broadcasted_iota(jnp.int32, (1, qb), 1)
    q_chunk = jnp.right_shift(q_idx, CHUNK_SHIFT)

    def key_index(kt):
        return kt * tk + lax.broadcasted_iota(jnp.int32, (tk, 1), 0)

    w = w_ref[0]

    def fold8(a):
        return a.reshape(tk // SUBLANES, SUBLANES, qb)

    def count8(off, pred):
        rows = tk // REDUCE_CHAINS
        parts = []
        for i in range(REDUCE_CHAINS):
            blk = sc_ref[pl.ds(off + i * rows, rows), :]
            ind = jnp.where(pred(blk, i * rows), 1.0, 0.0)
            parts.append(jnp.sum(ind.reshape(rows // SUBLANES, SUBLANES, qb), axis=0))
        while len(parts) > 1:
            parts = [parts[i] + parts[i + 1] for i in range(0, len(parts), 2)]
        return parts[0]

    def score_tile(kt, carry):
        smin, smax = carry
        off = pl.multiple_of(kt * tk, tk)
        ki = ki_ref[0, pl.ds(off, tk), :]
        sc = jnp.zeros((tk, qb), F32)
        for h in range(IDX_HEADS):
            s = _dot_nt(ki, qi_ref[0, h])
            sc = sc + w[h:h + 1] * jnp.maximum(s, 0.0)
        adm = jnp.right_shift(key_index(kt), CHUNK_SHIFT) <= q_chunk
        sc_ref[pl.ds(off, tk), :] = jnp.where(adm, sc, -jnp.inf)
        smin = jnp.minimum(smin, jnp.min(fold8(jnp.where(adm, sc, jnp.inf)), axis=0))
        smax = jnp.maximum(smax, jnp.max(fold8(jnp.where(adm, sc, -jnp.inf)), axis=0))
        return smin, smax

    smin8, smax8 = lax.fori_loop(0, n_kt, score_tile,
                                 (jnp.full((SUBLANES, qb), jnp.inf, F32),
                                  jnp.full((SUBLANES, qb), -jnp.inf, F32)))
    smin = jnp.min(smin8, axis=0, keepdims=True)
    smax = jnp.max(smax8, axis=0, keepdims=True)

    def count_ge(t):
        def body(kt, c8):
            off = pl.multiple_of(kt * tk, tk)
            return c8 + count8(off, lambda blk, r0: blk >= t)
        c8 = lax.fori_loop(0, n_kt, body, jnp.zeros((SUBLANES, qb), F32))
        return jnp.sum(c8, axis=0, keepdims=True)

    n_adm = (q_chunk + 1) * CHUNK
    take_all = n_adm <= n_sel
    nf = float(n_sel)

    def probe_step(st, t, may_converge):
        lo, hi, cnt_hi, thr, exact, done_f = st
        done = done_f > 0.0
        c = count_ge(t)
        inside = jnp.logical_and(t > lo, t < hi)
        conv = jnp.logical_and(may_converge, jnp.logical_not(inside))
        hit = jnp.logical_and(c == nf, jnp.logical_not(conv))
        ge = c >= nf
        new_done = jnp.logical_or(done, jnp.logical_or(hit, conv))
        thr_n = jnp.where(done, thr, jnp.where(hit, t, jnp.where(conv, lo, thr)))
        exact_n = jnp.where(done, exact, jnp.where(hit, 1.0, exact))
        upd = jnp.logical_not(new_done)
        move_lo = jnp.logical_and(upd, ge)
        move_hi = jnp.logical_and(upd, jnp.logical_not(ge))
        return (jnp.where(move_lo, t, lo), jnp.where(move_hi, t, hi), jnp.where(move_hi, c, cnt_hi),
                thr_n, exact_n, jnp.where(new_done, 1.0, 0.0))

    def midpoint(lo, hi):
        return lo + (hi - lo) * 0.5

    zeros = jnp.zeros((1, qb), F32)
    take_all_f = jnp.where(take_all, 1.0, 0.0)
    st = (smin, smax, zeros, jnp.full((1, qb), -jnp.inf, F32), take_all_f, take_all_f)
    lo, hi, cnt_hi, thr, exact, done_f = st
    c_top = count_ge(hi)
    top_hit = c_top == nf
    top_ge = c_top >= nf
    live = done_f == 0.0
    st = (jnp.where(jnp.logical_and(live, top_ge), hi, lo), hi,
          jnp.where(jnp.logical_and(live, jnp.logical_not(top_ge)), c_top, cnt_hi),
          jnp.where(jnp.logical_and(live, top_hit), hi, thr),
          jnp.where(jnp.logical_and(live, top_hit), 1.0, exact),
          jnp.where(jnp.logical_and(live, top_hit), 1.0, done_f))
    for special in (0.0, TINY_F32):
        lo, hi = st[0], st[1]
        sp = jnp.full((1, qb), special, F32)
        t = jnp.where(jnp.logical_and(sp > lo, sp < hi), sp, midpoint(lo, hi))
        st = probe_step(st, t, True)

    def bis_cond(st):
        return jnp.min(st[5]) < 1.0

    def bis_body(st):
        return probe_step(st, midpoint(st[0], st[1]), True)

    lo, hi, cnt_hi, thr, exact, _ = lax.while_loop(bis_cond, bis_body, st)
    need = nf - cnt_hi

    def count_tie(jcut):
        def body(kt, c8):
            off = pl.multiple_of(kt * tk, tk)

            def tied_before_cut(blk, r0):
                key = off + r0 + lax.broadcasted_iota(jnp.int32, (blk.shape[0], 1), 0)
                return jnp.logical_and(blk == thr, key <= jcut)
            return c8 + count8(off, tied_before_cut)
        c8 = lax.fori_loop(0, n_kt, body, jnp.zeros((SUBLANES, qb), F32))
        return jnp.sum(c8, axis=0, keepdims=True)

    def tie_cond(st):
        jlo, jhi = st
        return jnp.max(jnp.where(jnp.logical_and(exact == 0.0, jlo < jhi), 1.0, 0.0)) > 0.0

    def tie_body(st):
        jlo, jhi = st
        jmid = jnp.right_shift(jlo + jhi, 1)
        ok = count_tie(jmid) >= need
        return jnp.where(ok, jlo, jmid + 1), jnp.where(ok, jmid, jhi)

    jlo, _ = lax.while_loop(tie_cond, tie_body,
                            (jnp.zeros((1, qb), jnp.int32), jnp.full((1, qb), seq - 1, jnp.int32)))
    jcut = jnp.where(take_all, -1, jnp.where(exact > 0.0, seq, jlo))

    acc_ref[...] = jnp.zeros((D_ATTN, qb), F32)
    lane = lax.broadcasted_iota(jnp.int32, (qb, LANES), 1)
    for p in range(N_HEADS // 2):
        qp = q_ref[0, :, p * LANES:(p + 1) * LANES]
        qh_ref[2 * p] = jnp.where(lane < HEAD_DIM, qp, jnp.zeros_like(qp))
        qh_ref[2 * p + 1] = jnp.where(lane >= HEAD_DIM, qp, jnp.zeros_like(qp))

    def attn_tile(kt, carry):
        ms, ls = carry
        off = pl.multiple_of(kt * tk, tk)
        blk = sc_ref[pl.ds(off, tk), :]
        sel = jnp.logical_or(blk > thr, jnp.logical_and(blk == thr, key_index(kt) <= jcut))
        bias = jnp.where(sel, 0.0, NEG_BIG)
        logits = []
        for h in range(N_HEADS):
            p = h // 2
            kp = k_ref[0, pl.ds(off, tk), p * LANES:(p + 1) * LANES]
            logits.append(_dot_nt(kp, qh_ref[h]) + bias)
        m_new = [jnp.maximum(ms[h], jnp.max(logits[h], axis=0, keepdims=True)) for h in range(N_HEADS)]
        alpha = [jnp.exp(ms[h] - m_new[h]) for h in range(N_HEADS)]
        probs = [jnp.exp(logits[h] - m_new[h]) for h in range(N_HEADS)]
        l_new = [alpha[h] * ls[h] + jnp.sum(probs[h], axis=0, keepdims=True) for h in range(N_HEADS)]
        for h in range(N_HEADS):
            vt = vt_ref[0, h * HEAD_DIM:(h + 1) * HEAD_DIM, pl.ds(off, tk)]
            pv = _dot(vt, probs[h].astype(BF16))
            rows = slice(h * HEAD_DIM, (h + 1) * HEAD_DIM)
            acc_ref[rows, :] = alpha[h] * acc_ref[rows, :] + pv
        return tuple(m_new), tuple(l_new)

    m0 = tuple(jnp.full((1, qb), NEG_BIG, F32) for _ in range(N_HEADS))
    l0 = tuple(jnp.zeros((1, qb), F32) for _ in range(N_HEADS))
    _, ls = lax.fori_loop(0, n_kt, attn_tile, (m0, l0))
    for h in range(N_HEADS):
        rows = slice(h * HEAD_DIM, (h + 1) * HEAD_DIM)
        y_ref[0, rows, :] = acc_ref[rows, :] / ls[h]


def _dsa(ki_ext, k, vt, qi_ext, q, w_t, qb, tk):
    B, S, _ = k.shape
    n_sel = min(TOPK_KEYS_MAX, S // 4)
    resident = lambda shape: pl.BlockSpec(shape, lambda b, j: (b, 0, 0), pipeline_mode=pl.Buffered(1))
    return pl.pallas_call(
        functools.partial(_dsa_kernel, qb=qb, tk=tk, n_sel=n_sel, seq=S),
        grid=(B, S // qb),
        in_specs=[resident((1, S, IDX_K)), resident((1, S, D_ATTN)), resident((1, D_ATTN, S)),
                  pl.BlockSpec((1, IDX_HEADS, qb, IDX_K), lambda b, j: (b, 0, j, 0)),
                  pl.BlockSpec((1, qb, D_ATTN), lambda b, j: (b, j, 0)),
                  pl.BlockSpec((1, IDX_HEADS, qb), lambda b, j: (b, 0, j))],
        out_specs=pl.BlockSpec((1, D_ATTN, qb), lambda b, j: (b, 0, j)),
        out_shape=jax.ShapeDtypeStruct((B, D_ATTN, S), F32),
        scratch_shapes=[pltpu.VMEM((S, qb), F32), pltpu.VMEM((N_HEADS, qb, LANES), BF16),
                        pltpu.VMEM((D_ATTN, qb), F32)],
        compiler_params=_cparams("arbitrary", "arbitrary"),
        name="dsa",
    )(ki_ext, k, vt, qi_ext, q, w_t)


def _outproj_kernel(x_ref, yr_ref, ya_ref, mod_ref, ag_ref, wor_ref, woa_ref, n2g_ref,
                    wr_hi_ref, wr_lo_ref, br_ref,
                    x1_ref, h2_ref, meta_ref, cnt_ref, base_ref, *, tm):
    @pl.when(jnp.logical_and(pl.program_id(0) == 0, pl.program_id(1) == 0))
    def _():
        base_ref[...] = jnp.zeros((1, LANES), F32)

    mod = mod_ref[0]
    ya = _rms(ya_ref[0], ag_ref[...]).astype(BF16)
    mix = _dot(yr_ref[0], wor_ref[...]) + _dot(ya, woa_ref[...])
    x1 = x_ref[0] + mod[2:3] * mix
    x1_ref[0] = x1
    h2 = _rms(x1, n2g_ref[...]) * (1.0 + mod[4:5]) + mod[3:4]
    h2_ref[0] = h2.astype(BF16)
    h2_hi, h2_lo = _split(h2)
    logits = _dot3(h2_hi, h2_lo, wr_hi_ref[...], wr_lo_ref[...]) + br_ref[...]

    lane = lax.broadcasted_iota(jnp.int32, (tm, LANES), 1).astype(F32)
    work = logits
    vals, onehots, idxs = [], [], []
    for _ in range(TOPK_EXPERTS):
        mx = jnp.max(work, axis=-1, keepdims=True)
        idx = jnp.min(jnp.where(work == mx, lane, float(LANES)), axis=-1, keepdims=True)
        oh = lane == idx
        vals.append(mx)
        idxs.append(idx)
        onehots.append(oh)
        work = jnp.where(oh, -jnp.inf, work)
    es = [jnp.exp(v - vals[0]) for v in vals]
    denom = es[0] + es[1] + es[2] + es[3]
    assigned = jnp.zeros((tm, LANES), F32)
    for oh in onehots:
        assigned = assigned + jnp.where(oh, 1.0, 0.0)
    r_i = lax.broadcasted_iota(jnp.int32, (tm, tm), 0)
    c_i = lax.broadcasted_iota(jnp.int32, (tm, tm), 1)
    ltri = jnp.where(c_i < r_i, 1.0, 0.0).astype(BF16)
    prior = _dot(ltri, assigned.astype(BF16)) + base_ref[...]
    meta = jnp.zeros((tm, LANES), F32)
    for jj in range(TOPK_EXPERTS):
        rank = jnp.sum(jnp.where(onehots[jj], prior, 0.0), axis=-1, keepdims=True)
        meta = jnp.where(lane == jj, idxs[jj], meta)
        meta = jnp.where(lane == TOPK_EXPERTS + jj, es[jj] / denom, meta)
        meta = jnp.where(lane == 2 * TOPK_EXPERTS + jj, rank, meta)
    meta_ref[0] = meta
    base = base_ref[...] + jnp.sum(assigned, axis=0, keepdims=True)
    base_ref[...] = base
    cnt_ref[...] = base


def _outproj(x, y_rnn, y_attn, mod6, attn_out_g, w_out, norm2_g, w_router, b_router, tm):
    B, S, D = x.shape
    wor = w_out[:D_RNN].astype(BF16)
    woa = w_out[D_RNN:].astype(BF16)
    wr = jnp.zeros((D, LANES), F32).at[:, :N_EXPERTS].set(w_router)
    wr_hi, wr_lo = _split(wr)
    br = jnp.full((1, LANES), NEG_BIG, F32).at[0, :N_EXPERTS].set(b_router)
    const = lambda shape: pl.BlockSpec(shape, lambda b, s: (0,) * len(shape))
    row = lambda n: pl.BlockSpec((1, tm, n), lambda b, s: (b, s, 0))
    return pl.pallas_call(
        functools.partial(_outproj_kernel, tm=tm),
        grid=(B, S // tm),
        in_specs=[row(D), row(D_RNN), row(D_ATTN), pl.BlockSpec((1, 6, D), lambda b, s: (b, 0, 0)),
                  const((1, D_ATTN)), const((D_RNN, D)), const((D_ATTN, D)), const((1, D)),
                  const((D, LANES)), const((D, LANES)), const((1, LANES))],
        out_specs=[row(D), row(D), row(LANES), const((1, LANES))],
        out_shape=[jax.ShapeDtypeStruct((B, S, D), F32), jax.ShapeDtypeStruct((B, S, D), BF16),
                   jax.ShapeDtypeStruct((B, S, LANES), F32), jax.ShapeDtypeStruct((1, LANES), F32)],
        scratch_shapes=[pltpu.VMEM((1, LANES), F32)],
        compiler_params=_cparams("arbitrary", "arbitrary"),
        name="outproj",
    )(x, y_rnn, y_attn, mod6, attn_out_g.reshape(1, D_ATTN), wor, woa, norm2_g.reshape(1, D),
      wr_hi, wr_lo, br)


GLU_GROUP = 2 * LANES


def _degroup_kernel(w_ref, p_ref, o_ref):
    p = p_ref[...]
    for g in range(2 * D_FF // GLU_GROUP):
        cols = slice(g * GLU_GROUP, (g + 1) * GLU_GROUP)
        o_ref[0, :, cols] = _dot(w_ref[0, :, cols].astype(BF16), p).astype(BF16)


def _degroup_w1(w1):
    E, D, F2 = w1.shape
    src = jnp.arange(GLU_GROUP)
    dst = jnp.where(src % 2 == 0, src // 2, LANES + src // 2)
    perm = (dst[:, None] == jnp.arange(GLU_GROUP)[None, :]).astype(BF16)
    tr = 512
    return pl.pallas_call(
        _degroup_kernel,
        grid=(E, D // tr),
        in_specs=[pl.BlockSpec((1, tr, F2), lambda e, r: (e, r, 0)),
                  pl.BlockSpec((GLU_GROUP, GLU_GROUP), lambda e, r: (0, 0))],
        out_specs=pl.BlockSpec((1, tr, F2), lambda e, r: (e, r, 0)),
        out_shape=jax.ShapeDtypeStruct((E, D, F2), BF16),
        compiler_params=_cparams("arbitrary", "arbitrary"),
        name="degroup_w1",
    )(w1, perm)


def _expert_kernel(be_ref, x_ref, w1_ref, b1_ref, w2_ref, b2_ref, y_ref):
    del be_ref
    u = _dot(x_ref[...], w1_ref[0]) + b1_ref[0]
    acts = []
    for g in range(2 * D_FF // GLU_GROUP):
        ug = jnp.minimum(u[:, g * GLU_GROUP:g * GLU_GROUP + LANES], SWIGLU_LIMIT)
        ul = jnp.clip(u[:, g * GLU_GROUP + LANES:(g + 1) * GLU_GROUP], -SWIGLU_LIMIT, SWIGLU_LIMIT)
        acts.append((ug * jax.nn.sigmoid(SWIGLU_ALPHA * ug) * (ul + 1.0)).astype(BF16))
    act = jnp.concatenate(acts, axis=1)
    y_ref[...] = _dot(act, w2_ref[0]) + b2_ref[0]


def _experts(x_sorted, block_e, w1, b1, w2, b2, mb):
    n_rows, D = x_sorted.shape
    E = w1.shape[0]
    w1p = _degroup_w1(w1)
    ng = 2 * D_FF // GLU_GROUP
    b1p = jnp.swapaxes(b1.reshape(E, ng, LANES, 2), 2, 3).reshape(E, 1, 2 * D_FF)
    wspec = lambda k, n: pl.BlockSpec((1, k, n), lambda i, be: (be[i], 0, 0))
    grid_spec = pltpu.PrefetchScalarGridSpec(
        num_scalar_prefetch=1,
        grid=(n_rows // mb,),
        in_specs=[pl.BlockSpec((mb, D), lambda i, be: (i, 0)),
                  wspec(D, 2 * D_FF), wspec(1, 2 * D_FF), wspec(D_FF, D), wspec(1, D)],
        out_specs=pl.BlockSpec((mb, D), lambda i, be: (i, 0)),
    )
    return pl.pallas_call(
        _expert_kernel,
        grid_spec=grid_spec,
        out_shape=jax.ShapeDtypeStruct((n_rows, D), F32),
        compiler_params=_cparams("arbitrary"),
        name="experts",
    )(block_e, x_sorted, w1p, b1p, w2.astype(BF16), b2.reshape(E, 1, D))


def _combine_kernel(x1_ref, yg_ref, meta_ref, mod_ref, o_ref):
    meta = meta_ref[0]
    ff = meta[:, TOPK_EXPERTS:TOPK_EXPERTS + 1] * yg_ref[0]
    for jj in range(1, TOPK_EXPERTS):
        ff = ff + meta[:, TOPK_EXPERTS + jj:TOPK_EXPERTS + jj + 1] * yg_ref[jj]
    o_ref[0] = x1_ref[0] + mod_ref[0][5:6] * ff


def _combine(x1, yg, meta, mod6, tm):
    B, S, D = x1.shape
    nt = S // tm
    return pl.pallas_call(
        _combine_kernel,
        grid=(B, nt),
        in_specs=[pl.BlockSpec((1, tm, D), lambda b, s: (b, s, 0)),
                  pl.BlockSpec((TOPK_EXPERTS, tm, D), lambda b, s: (0, b * nt + s, 0)),
                  pl.BlockSpec((1, tm, LANES), lambda b, s: (b, s, 0)),
                  pl.BlockSpec((1, 6, D), lambda b, s: (b, 0, 0))],
        out_specs=pl.BlockSpec((1, tm, D), lambda b, s: (b, s, 0)),
        out_shape=jax.ShapeDtypeStruct((B, S, D), F32),
        compiler_params=_cparams("arbitrary", "arbitrary"),
        name="combine",
    )(x1, yg, meta, mod6)


def _tile(n, pref):
    t = pref
    while n % t:
        t //= 2
    return t


def _layer(x, c, w_ada, b_ada, norm1_g, w_in, conv_w, conv_b, w_rg_a, b_rg_a, w_rg_x, b_rg_x,
           lru_lambda, q_norm_g, k_norm_g, kidx_norm_g, rg_out_g, attn_out_g, w_out, norm2_g,
           w_router, b_router, w1, b1, w2, b2):
    B, S, D = x.shape
    T = B * S
    tm = _tile(S, 512)
    mod6 = _ada(c, w_ada, b_ada).reshape(B, 6, D)

    xr, xg, q, k, v, qi, kiw = _inproj(x, mod6, norm1_g, w_in, q_norm_g, k_norm_g, kidx_norm_g, tm)
    y_rnn = _rglru(xr, xg, conv_w, conv_b, w_rg_a, b_rg_a, w_rg_x, b_rg_x, lru_lambda, rg_out_g,
                   _tile(S, 256))

    ki = kiw[..., :IDX_DIM]
    ki_hi, ki_lo = _split(ki)
    ki_ext = jnp.concatenate([ki_hi, ki_hi, ki_lo, jnp.zeros_like(ki_hi)], axis=-1)
    qi_h = jnp.swapaxes(qi.reshape(B, S, IDX_HEADS, IDX_DIM), 1, 2)
    qi_hi, qi_lo = _split(qi_h)
    qi_ext = jnp.concatenate([qi_hi, qi_lo, qi_hi, jnp.zeros_like(qi_hi)], axis=-1)
    w_t = jnp.swapaxes(kiw[..., IDX_DIM:IDX_DIM + IDX_HEADS], 1, 2)
    vt = jnp.swapaxes(v, 1, 2)
    qb = _tile(S, 256)
    y_attn_t = _dsa(ki_ext, k, vt, qi_ext, q, w_t, qb, _tile(S, 512))
    y_attn = jnp.swapaxes(y_attn_t, 1, 2)

    x1, h2, meta, cnt = _outproj(x, y_rnn, y_attn, mod6, attn_out_g, w_out, norm2_g,
                                 w_router, b_router, tm)

    mb = 256
    top_e = meta[..., 0:TOPK_EXPERTS].astype(jnp.int32).reshape(T, TOPK_EXPERTS)
    rank = meta[..., 2 * TOPK_EXPERTS:3 * TOPK_EXPERTS].astype(jnp.int32).reshape(T, TOPK_EXPERTS)
    counts = cnt[0, :N_EXPERTS].astype(jnp.int32)
    padded = (counts + mb - 1) // mb * mb
    pad_end = jnp.cumsum(padded)
    pad_start = pad_end - padded
    dest = pad_start[top_e] + rank
    n_rows = T * TOPK_EXPERTS + N_EXPERTS * mb
    n_blocks = n_rows // mb
    block_start = jnp.arange(n_blocks, dtype=jnp.int32) * mb
    block_e = jnp.minimum(jnp.sum((pad_end[None, :] <= block_start[:, None]).astype(jnp.int32), axis=1),
                          N_EXPERTS - 1)
    tok = jnp.broadcast_to(jnp.arange(T, dtype=jnp.int32)[:, None], (T, TOPK_EXPERTS))
    row_tok = jnp.zeros((n_rows,), jnp.int32).at[dest.reshape(-1)].set(tok.reshape(-1))
    x_sorted = h2.reshape(T, D)[row_tok]
    y_sorted = _experts(x_sorted, block_e, w1, b1, w2, b2, mb)
    yg = y_sorted[dest.T]
    return _combine(x1, yg, meta, mod6, tm)


def kernel(x, c, w_ada, b_ada, norm1_g, w_in, conv_w, conv_b, w_rg_a, b_rg_a, w_rg_x, b_rg_x,
           lru_lambda, q_norm_g, k_norm_g, kidx_norm_g, rg_out_g, attn_out_g, w_out, norm2_g,
           w_router, b_router, w1, b1, w2, b2):
    depth = w_ada.shape[0]
    for l in range(depth):
        x = _layer(x, c, w_ada[l], b_ada[l], norm1_g[l], w_in[l], conv_w[l], conv_b[l],
                   w_rg_a[l], b_rg_a[l], w_rg_x[l], b_rg_x[l], lru_lambda[l], q_norm_g[l],
                   k_norm_g[l], kidx_norm_g[l], rg_out_g[l], attn_out_g[l], w_out[l], norm2_g[l],
                   w_router[l], b_router[l], w1[l], b1[l], w2[l], b2[l])
    return x
```

```python
import functools

import jax
import jax.numpy as jnp
from jax import lax
from jax.experimental import pallas as pl
from jax.experimental.pallas import tpu as pltpu

D_MODEL = 1024
CHUNK = 64
CHUNK_SHIFT = 6
D_RNN = 512
RNN_BLOCKS = 8
RNN_BLOCK = 64
CONV_WIDTH = 4
RG_C = 8.0
N_HEADS = 8
HEAD_DIM = 64
D_ATTN = 512
IDX_HEADS = 8
IDX_DIM = 64
TOPK_KEYS_MAX = 256
N_EXPERTS = 32
TOPK_EXPERTS = 4
D_FF = 1024
SWIGLU_ALPHA = 1.702
SWIGLU_LIMIT = 7.0
EPS = 1e-6

LANES = 128
SUBLANES = 8
TINY_F32 = 1.1754943508222875e-38
REDUCE_CHAINS = 8
IDX_K = 256
NEG_BIG = -(2.0 ** 100)
LOG2E = 1.4426950408889634
V_ROWS = 80
VMEM_LIMIT = 56 * 1024 * 1024

F32 = jnp.float32
BF16 = jnp.bfloat16


def _dot(a, b):
    return jnp.dot(a, b, preferred_element_type=F32)


def _dot_nt(a, b):
    return lax.dot_general(a, b, (((1,), (1,)), ((), ())), preferred_element_type=F32)


def _split(a):
    hi = a.astype(BF16)
    lo = (a - hi.astype(F32)).astype(BF16)
    return hi, lo


def _dot3(a_hi, a_lo, b_hi, b_lo):
    return _dot(a_hi, b_hi) + (_dot(a_hi, b_lo) + _dot(a_lo, b_hi))


def _rms(x, g):
    ms = jnp.mean(x * x, axis=-1, keepdims=True)
    return x * lax.rsqrt(ms + EPS) * g


def _cparams(*sem):
    return pltpu.CompilerParams(dimension_semantics=sem, vmem_limit_bytes=VMEM_LIMIT)


def _ada_kernel(c_ref, w_ref, b_ref, o_ref):
    c = c_ref[...]
    s = c * jax.nn.sigmoid(c)
    s_hi, s_lo = _split(s)
    w_hi, w_lo = _split(w_ref[...])
    o_ref[...] = _dot3(s_hi, s_lo, w_hi, w_lo) + b_ref[...]


def _ada(c, w_ada, b_ada):
    B, D = c.shape
    N = w_ada.shape[1]
    rows = 16
    c_pad = jnp.zeros((rows, D), F32).at[:B].set(c)
    tn = 1024
    out = pl.pallas_call(
        _ada_kernel,
        grid=(N // tn,),
        in_specs=[pl.BlockSpec((rows, D), lambda j: (0, 0)),
                  pl.BlockSpec((D, tn), lambda j: (0, j)),
                  pl.BlockSpec((1, tn), lambda j: (0, j))],
        out_specs=pl.BlockSpec((rows, tn), lambda j: (0, j)),
        out_shape=jax.ShapeDtypeStruct((rows, N), F32),
        compiler_params=_cparams("arbitrary"),
        name="ada",
    )(c_pad, w_ada, b_ada.reshape(1, N))
    return out[:B]


def _inproj_kernel(x_ref, mod_ref, g1_ref, wmain_ref, widx_hi_ref, widx_lo_ref, bd_ref,
                   qg_ref, kg_ref, kig_ref,
                   xr_ref, xg_ref, q_ref, k_ref, v_ref, qi_ref, kiw_ref):
    x = x_ref[0]
    mod = mod_ref[0]
    h = _rms(x, g1_ref[...]) * (1.0 + mod[1:2]) + mod[0:1]
    h_hi, h_lo = _split(h)
    bd = bd_ref[...]

    def head_norm(z, g):
        sq_hi, sq_lo = _split(z * z)
        ms = _dot(sq_hi, bd) + _dot(sq_lo, bd)
        return z * lax.rsqrt(ms + EPS) * g

    xr_ref[0] = _dot(h_hi, wmain_ref[:, 0:512])
    xg_ref[0] = _dot(h_hi, wmain_ref[:, 512:1024])
    zq = _dot(h_hi, wmain_ref[:, 1024:1536])
    q_ref[0] = (head_norm(zq, qg_ref[...]) * (HEAD_DIM ** -0.5 * LOG2E)).astype(BF16)
    zk = _dot(h_hi, wmain_ref[:, 1536:2048])
    k_ref[0] = head_norm(zk, kg_ref[...]).astype(BF16)
    v_ref[0] = _dot(h_hi, wmain_ref[:, 2048:2560]).astype(BF16)

    zi = _dot3(h_hi, h_lo, widx_hi_ref[...], widx_lo_ref[...])
    qi_ref[0] = zi[:, 0:512] * (IDX_DIM ** -0.5)
    kw = zi[:, 512:640]
    lane = lax.broadcasted_iota(jnp.int32, kw.shape, 1)
    is_ki = lane < IDX_DIM
    ms = jnp.sum(jnp.where(is_ki, kw * kw, 0.0), axis=-1, keepdims=True) * (1.0 / IDX_DIM)
    kiw_ref[0] = jnp.where(is_ki, kw * lax.rsqrt(ms + EPS) * kig_ref[...], kw * (IDX_HEADS ** -0.5))


def _inproj(x, mod6, norm1_g, w_in, q_norm_g, k_norm_g, kidx_norm_g, tm):
    B, S, D = x.shape
    wmain = w_in[:, :2560].astype(BF16)
    widx = jnp.concatenate([w_in[:, 2560:3144], jnp.zeros((D, 640 - 584), F32)], axis=1)
    widx_hi = widx.astype(BF16)
    widx_lo = (widx - widx_hi.astype(F32)).astype(BF16)
    head = jnp.arange(D_ATTN) // HEAD_DIM
    bd = jnp.where(head[:, None] == head[None, :], 1.0 / HEAD_DIM, 0.0).astype(BF16)
    qg = jnp.tile(q_norm_g, N_HEADS).reshape(1, D_ATTN)
    kg = jnp.tile(k_norm_g, N_HEADS).reshape(1, D_ATTN)
    kig = jnp.concatenate([kidx_norm_g, jnp.zeros((LANES - IDX_DIM,), F32)]).reshape(1, LANES)
    const = lambda shape: pl.BlockSpec(shape, lambda b, s: (0,) * len(shape))
    row = lambda n: pl.BlockSpec((1, tm, n), lambda b, s: (b, s, 0))
    return pl.pallas_call(
        _inproj_kernel,
        grid=(B, S // tm),
        in_specs=[row(D), pl.BlockSpec((1, 6, D), lambda b, s: (b, 0, 0)), const((1, D)),
                  const((D, 2560)), const((D, 640)), const((D, 640)), const((D_ATTN, D_ATTN)),
                  const((1, D_ATTN)), const((1, D_ATTN)), const((1, LANES))],
        out_specs=[row(512), row(512), row(512), row(512), row(512), row(512), row(LANES)],
        out_shape=[jax.ShapeDtypeStruct((B, S, 512), F32), jax.ShapeDtypeStruct((B, S, 512), F32),
                   jax.ShapeDtypeStruct((B, S, 512), BF16), jax.ShapeDtypeStruct((B, S, 512), BF16),
                   jax.ShapeDtypeStruct((B, S, 512), BF16), jax.ShapeDtypeStruct((B, S, 512), F32),
                   jax.ShapeDtypeStruct((B, S, LANES), F32)],
        compiler_params=_cparams("arbitrary", "arbitrary"),
        name="inproj",
    )(x, mod6, norm1_g.reshape(1, D), wmain, widx_hi, widx_lo, bd, qg, kg, kig)


def _rglru_kernel(xr_ref, xg_ref, cw_ref, cb_ref, wa_hi_ref, wa_lo_ref, wx_hi_ref, wx_lo_ref,
                  ba_ref, bx_ref, lam_ref, g_ref, y_ref, ext_ref, hlast_ref, *, ts):
    @pl.when(pl.program_id(1) == 0)
    def _():
        ext_ref[0:8, :] = jnp.zeros((8, D_RNN), F32)
        hlast_ref[...] = jnp.zeros((1, D_RNN), F32)

    xr = xr_ref[0]
    ext_ref[8:8 + ts, :] = xr
    cw = cw_ref[...]
    xc = cb_ref[...] + cw[0:1] * ext_ref[5:5 + ts, :]
    for j in range(1, CONV_WIDTH):
        xc = xc + cw[j:j + 1] * ext_ref[5 + j:5 + j + ts, :]
    ext_ref[0:8, :] = xr[ts - 8:ts]

    xc_hi, xc_lo = _split(xc)
    r = jax.nn.sigmoid(_dot3(xc_hi, xc_lo, wa_hi_ref[...], wa_lo_ref[...]) + ba_ref[...])
    i = jax.nn.sigmoid(_dot3(xc_hi, xc_lo, wx_hi_ref[...], wx_lo_ref[...]) + bx_ref[...])
    lam = lam_ref[...]
    sp = jnp.maximum(-lam, 0.0) + jnp.log1p(jnp.exp(-jnp.abs(lam)))
    log_a = (-RG_C) * r * sp
    a = jnp.exp(log_a)
    mult = jnp.sqrt(-jnp.tanh(log_a) * (a * a + 1.0))
    u = mult * i * xc

    row = lax.broadcasted_iota(jnp.int32, (ts, D_RNN), 0)
    acc_a, acc_h = a, u
    d = 1
    while d < ts:
        sh_a = pltpu.roll(acc_a, d, 0)
        sh_h = pltpu.roll(acc_h, d, 0)
        ok = row >= d
        acc_h = jnp.where(ok, acc_a * sh_h + acc_h, acc_h)
        acc_a = jnp.where(ok, acc_a * sh_a, acc_a)
        d *= 2
    hseq = acc_h + acc_a * hlast_ref[...]
    hlast_ref[...] = hseq[ts - 1:ts]

    xg = xg_ref[0]
    gelu = 0.5 * xg * (1.0 + jnp.tanh(0.7978845608028654 * (xg + 0.044715 * (xg * xg * xg))))
    y_ref[0] = _rms(gelu * hseq, g_ref[...]).astype(BF16)


def _block_diag(w):
    H, n, _ = w.shape
    eye = jnp.eye(H, dtype=w.dtype)
    return (eye[:, None, :, None] * w[:, :, None, :]).reshape(H * n, H * n)


def _rglru(xr, xg, conv_w, conv_b, w_a, b_a, w_x, b_x, lam, g, ts):
    B, S, _ = xr.shape
    wa_hi, wa_lo = _split(_block_diag(w_a))
    wx_hi, wx_lo = _split(_block_diag(w_x))
    const = lambda shape: pl.BlockSpec(shape, lambda b, s: (0,) * len(shape))
    row = pl.BlockSpec((1, ts, D_RNN), lambda b, s: (b, s, 0))
    vec = lambda a: a.reshape(1, D_RNN)
    return pl.pallas_call(
        functools.partial(_rglru_kernel, ts=ts),
        grid=(B, S // ts),
        in_specs=[row, row, const((CONV_WIDTH, D_RNN)), const((1, D_RNN)),
                  const((D_RNN, D_RNN)), const((D_RNN, D_RNN)), const((D_RNN, D_RNN)), const((D_RNN, D_RNN)),
                  const((1, D_RNN)), const((1, D_RNN)), const((1, D_RNN)), const((1, D_RNN))],
        out_specs=row,
        out_shape=jax.ShapeDtypeStruct((B, S, D_RNN), BF16),
        scratch_shapes=[pltpu.VMEM((ts + 8, D_RNN), F32), pltpu.VMEM((1, D_RNN), F32)],
        compiler_params=_cparams("arbitrary", "arbitrary"),
        name="rglru",
    )(xr, xg, conv_w, vec(conv_b), wa_hi, wa_lo, wx_hi, wx_lo, vec(b_a), vec(b_x), vec(lam), vec(g))


def _dsa_kernel(ki_ref, k_ref, vt_ref, qi_ref, q_ref, w_ref, y_ref,
                sc_ref, qh_ref, acc_ref, *, qb, tk, n_sel, seq):
    j = pl.program_id(1)
    n_kt = ((j + 1) * qb + tk - 1) // tk
    q_idx = j * qb + lax.broadcasted_iota(jnp.int32, (1, qb), 1)
    q_chunk = jnp.right_shift(q_idx, CHUNK_SHIFT)

    def key_index(kt):
        return kt * tk + lax.broadcasted_iota(jnp.int32, (tk, 1), 0)

    w = w_ref[0]

    def fold8(a):
        return a.reshape(tk // SUBLANES, SUBLANES, qb)

    def count8(off, pred):
        rows = tk // REDUCE_CHAINS
        parts = []
        for i in range(REDUCE_CHAINS):
            blk = sc_ref[pl.ds(off + i * rows, rows), :]
            ind = jnp.where(pred(blk, i * rows), 1.0, 0.0)
            parts.append(jnp.sum(ind.reshape(rows // SUBLANES, SUBLANES, qb), axis=0))
        while len(parts) > 1:
            parts = [parts[i] + parts[i + 1] for i in range(0, len(parts), 2)]
        return parts[0]

    def score_tile(kt, carry):
        smin, smax = carry
        off = pl.multiple_of(kt * tk, tk)
        ki = ki_ref[0, pl.ds(off, tk), :]
        sc = jnp.zeros((tk, qb), F32)
        for h in range(IDX_HEADS):
            s = _dot_nt(ki, qi_ref[0, h])
            sc = sc + w[h:h + 1] * jnp.maximum(s, 0.0)
        adm = jnp.right_shift(key_index(kt), CHUNK_SHIFT) <= q_chunk
        sc_ref[pl.ds(off, tk), :] = jnp.where(adm, sc, -jnp.inf)
        smin = jnp.minimum(smin, jnp.min(fold8(jnp.where(adm, sc, jnp.inf)), axis=0))
        smax = jnp.maximum(smax, jnp.max(fold8(jnp.where(adm, sc, -jnp.inf)), axis=0))
        return smin, smax

    smin8, smax8 = lax.fori_loop(0, n_kt, score_tile,
                                 (jnp.full((SUBLANES, qb), jnp.inf, F32),
                                  jnp.full((SUBLANES, qb), -jnp.inf, F32)))
    smin = jnp.min(smin8, axis=0, keepdims=True)
    smax = jnp.max(smax8, axis=0, keepdims=True)

    def count_ge(t):
        def body(kt, c8):
            off = pl.multiple_of(kt * tk, tk)
            return c8 + count8(off, lambda blk, r0: blk >= t)
        c8 = lax.fori_loop(0, n_kt, body, jnp.zeros((SUBLANES, qb), F32))
        return jnp.sum(c8, axis=0, keepdims=True)

    n_adm = (q_chunk + 1) * CHUNK
    take_all = n_adm <= n_sel
    nf = float(n_sel)

    def probe_step(st, t, may_converge):
        lo, hi, cnt_hi, thr, exact, done_f = st
        done = done_f > 0.0
        c = count_ge(t)
        inside = jnp.logical_and(t > lo, t < hi)
        conv = jnp.logical_and(may_converge, jnp.logical_not(inside))
        hit = jnp.logical_and(c == nf, jnp.logical_not(conv))
        ge = c >= nf
        new_done = jnp.logical_or(done, jnp.logical_or(hit, conv))
        thr_n = jnp.where(done, thr, jnp.where(hit, t, jnp.where(conv, lo, thr)))
        exact_n = jnp.where(done, exact, jnp.where(hit, 1.0, exact))
        upd = jnp.logical_not(new_done)
        move_lo = jnp.logical_and(upd, ge)
        move_hi = jnp.logical_and(upd, jnp.logical_not(ge))
        return (jnp.where(move_lo, t, lo), jnp.where(move_hi, t, hi), jnp.where(move_hi, c, cnt_hi),
                thr_n, exact_n, jnp.where(new_done, 1.0, 0.0))

    def midpoint(lo, hi):
        return lo + (hi - lo) * 0.5

    zeros = jnp.zeros((1, qb), F32)
    take_all_f = jnp.where(take_all, 1.0, 0.0)
    st = (smin, smax, zeros, jnp.full((1, qb), -jnp.inf, F32), take_all_f, take_all_f)
    lo, hi, cnt_hi, thr, exact, done_f = st
    c_top = count_ge(hi)
    top_hit = c_top == nf
    top_ge = c_top >= nf
    live = done_f == 0.0
    st = (jnp.where(jnp.logical_and(live, top_ge), hi, lo), hi,
          jnp.where(jnp.logical_and(live, jnp.logical_not(top_ge)), c_top, cnt_hi),
          jnp.where(jnp.logical_and(live, top_hit), hi, thr),
          jnp.where(jnp.logical_and(live, top_hit), 1.0, exact),
          jnp.where(jnp.logical_and(live, top_hit), 1.0, done_f))
    for special in (0.0, TINY_F32):
        lo, hi = st[0], st[1]
        sp = jnp.full((1, qb), special, F32)
        t = jnp.where(jnp.logical_and(sp > lo, sp < hi), sp, midpoint(lo, hi))
        st = probe_step(st, t, True)

    def bis_cond(st):
        return jnp.min(st[5]) < 1.0

    def bis_body(st):
        return probe_step(st, midpoint(st[0], st[1]), True)

    lo, hi, cnt_hi, thr, exact, _ = lax.while_loop(bis_cond, bis_body, st)
    need = nf - cnt_hi

    def count_tie(jcut):
        def body(kt, c8):
            off = pl.multiple_of(kt * tk, tk)

            def tied_before_cut(blk, r0):
                key = off + r0 + lax.broadcasted_iota(jnp.int32, (blk.shape[0], 1), 0)
                return jnp.logical_and(blk == thr, key <= jcut)
            return c8 + count8(off, tied_before_cut)
        c8 = lax.fori_loop(0, n_kt, body, jnp.zeros((SUBLANES, qb), F32))
        return jnp.sum(c8, axis=0, keepdims=True)

    def tie_cond(st):
        jlo, jhi = st
        return jnp.max(jnp.where(jnp.logical_and(exact == 0.0, jlo < jhi), 1.0, 0.0)) > 0.0

    def tie_body(st):
        jlo, jhi = st
        jmid = jnp.right_shift(jlo + jhi, 1)
        ok = count_tie(jmid) >= need
        return jnp.where(ok, jlo, jmid + 1), jnp.where(ok, jmid, jhi)

    jlo, _ = lax.while_loop(tie_cond, tie_body,
                            (jnp.zeros((1, qb), jnp.int32), jnp.full((1, qb), seq - 1, jnp.int32)))
    jcut = jnp.where(take_all, -1, jnp.where(exact > 0.0, seq, jlo))

    acc_ref[...] = jnp.zeros((N_HEADS * V_ROWS, qb), F32)
    lane = lax.broadcasted_iota(jnp.int32, (qb, LANES), 1)
    for p in range(N_HEADS // 2):
        qp = q_ref[0, :, p * LANES:(p + 1) * LANES]
        qh_ref[2 * p] = jnp.where(lane < HEAD_DIM, qp, jnp.zeros_like(qp))
        qh_ref[2 * p + 1] = jnp.where(lane >= HEAD_DIM, qp, jnp.zeros_like(qp))

    def attn_tile(kt, ms):
        off = pl.multiple_of(kt * tk, tk)
        blk = sc_ref[pl.ds(off, tk), :]
        sel = jnp.logical_or(blk > thr, jnp.logical_and(blk == thr, key_index(kt) <= jcut))
        bias = jnp.where(sel, 0.0, NEG_BIG).astype(BF16)
        logits = []
        for h in range(N_HEADS):
            p = h // 2
            kp = k_ref[0, pl.ds(off, tk), p * LANES:(p + 1) * LANES]
            logits.append(_dot_nt(kp, qh_ref[h]).astype(BF16) + bias)
        m_new = []
        for h in range(N_HEADS):
            m_h = jnp.maximum(ms[h], jnp.max(logits[h], axis=0, keepdims=True).astype(F32))
            alpha = jnp.exp2(ms[h] - m_h)
            probs = jnp.exp2(logits[h] - m_h.astype(BF16))
            vt = vt_ref[0, h * V_ROWS:(h + 1) * V_ROWS, pl.ds(off, tk)]
            rows = slice(h * V_ROWS, (h + 1) * V_ROWS)
            acc_ref[rows, :] = alpha * acc_ref[rows, :] + _dot(vt, probs)
            m_new.append(m_h)
        return tuple(m_new)

    m0 = tuple(jnp.full((1, qb), NEG_BIG, F32) for _ in range(N_HEADS))
    lax.fori_loop(0, n_kt, attn_tile, m0)
    for h in range(N_HEADS):
        num = acc_ref[h * V_ROWS:h * V_ROWS + HEAD_DIM, :]
        den = acc_ref[h * V_ROWS + HEAD_DIM:h * V_ROWS + HEAD_DIM + 1, :]
        y_ref[0, h * HEAD_DIM:(h + 1) * HEAD_DIM, :] = num / den


def _dsa(ki_ext, k, vt, qi_ext, q, w_t, qb, tk):
    B, S, _ = k.shape
    n_sel = min(TOPK_KEYS_MAX, S // 4)
    resident = lambda shape: pl.BlockSpec(shape, lambda b, j: (b, 0, 0), pipeline_mode=pl.Buffered(1))
    return pl.pallas_call(
        functools.partial(_dsa_kernel, qb=qb, tk=tk, n_sel=n_sel, seq=S),
        grid=(B, S // qb),
        in_specs=[resident((1, S, IDX_K)), resident((1, S, D_ATTN)), resident((1, N_HEADS * V_ROWS, S)),
                  pl.BlockSpec((1, IDX_HEADS, qb, IDX_K), lambda b, j: (b, 0, j, 0)),
                  pl.BlockSpec((1, qb, D_ATTN), lambda b, j: (b, j, 0)),
                  pl.BlockSpec((1, IDX_HEADS, qb), lambda b, j: (b, 0, j))],
        out_specs=pl.BlockSpec((1, D_ATTN, qb), lambda b, j: (b, 0, j)),
        out_shape=jax.ShapeDtypeStruct((B, D_ATTN, S), F32),
        scratch_shapes=[pltpu.VMEM((S, qb), F32), pltpu.VMEM((N_HEADS, qb, LANES), BF16),
                        pltpu.VMEM((N_HEADS * V_ROWS, qb), F32)],
        compiler_params=_cparams("arbitrary", "arbitrary"),
        name="dsa",
    )(ki_ext, k, vt, qi_ext, q, w_t)


def _outproj_kernel(x_ref, yr_ref, ya_ref, mod_ref, ag_ref, wor_ref, woa_ref, n2g_ref,
                    wr_hi_ref, wr_lo_ref, br_ref,
                    x1_ref, h2_ref, meta_ref, cnt_ref, base_ref, *, tm):
    @pl.when(jnp.logical_and(pl.program_id(0) == 0, pl.program_id(1) == 0))
    def _():
        base_ref[...] = jnp.zeros((1, LANES), F32)

    mod = mod_ref[0]
    ya = _rms(ya_ref[0], ag_ref[...]).astype(BF16)
    mix = _dot(yr_ref[0], wor_ref[...]) + _dot(ya, woa_ref[...])
    x1 = x_ref[0] + mod[2:3] * mix
    x1_ref[0] = x1
    h2 = _rms(x1, n2g_ref[...]) * (1.0 + mod[4:5]) + mod[3:4]
    h2_ref[0] = h2.astype(BF16)
    h2_hi, h2_lo = _split(h2)
    logits = _dot3(h2_hi, h2_lo, wr_hi_ref[...], wr_lo_ref[...]) + br_ref[...]

    lane = lax.broadcasted_iota(jnp.int32, (tm, LANES), 1).astype(F32)
    work = logits
    vals, onehots, idxs = [], [], []
    for _ in range(TOPK_EXPERTS):
        mx = jnp.max(work, axis=-1, keepdims=True)
        idx = jnp.min(jnp.where(work == mx, lane, float(LANES)), axis=-1, keepdims=True)
        oh = lane == idx
        vals.append(mx)
        idxs.append(idx)
        onehots.append(oh)
        work = jnp.where(oh, -jnp.inf, work)
    es = [jnp.exp(v - vals[0]) for v in vals]
    denom = es[0] + es[1] + es[2] + es[3]
    assigned = jnp.zeros((tm, LANES), F32)
    for oh in onehots:
        assigned = assigned + jnp.where(oh, 1.0, 0.0)
    r_i = lax.broadcasted_iota(jnp.int32, (tm, tm), 0)
    c_i = lax.broadcasted_iota(jnp.int32, (tm, tm), 1)
    ltri = jnp.where(c_i < r_i, 1.0, 0.0).astype(BF16)
    prior = _dot(ltri, assigned.astype(BF16)) + base_ref[...]
    meta = jnp.zeros((tm, LANES), F32)
    for jj in range(TOPK_EXPERTS):
        rank = jnp.sum(jnp.where(onehots[jj], prior, 0.0), axis=-1, keepdims=True)
        meta = jnp.where(lane == jj, idxs[jj], meta)
        meta = jnp.where(lane == TOPK_EXPERTS + jj, es[jj] / denom, meta)
        meta = jnp.where(lane == 2 * TOPK_EXPERTS + jj, rank, meta)
    meta_ref[0] = meta
    base = base_ref[...] + jnp.sum(assigned, axis=0, keepdims=True)
    base_ref[...] = base
    cnt_ref[...] = base


def _outproj(x, y_rnn, y_attn, mod6, attn_out_g, w_out, norm2_g, w_router, b_router, tm):
    B, S, D = x.shape
    wor = w_out[:D_RNN].astype(BF16)
    woa = w_out[D_RNN:].astype(BF16)
    wr = jnp.zeros((D, LANES), F32).at[:, :N_EXPERTS].set(w_router)
    wr_hi, wr_lo = _split(wr)
    br = jnp.full((1, LANES), NEG_BIG, F32).at[0, :N_EXPERTS].set(b_router)
    const = lambda shape: pl.BlockSpec(shape, lambda b, s: (0,) * len(shape))
    row = lambda n: pl.BlockSpec((1, tm, n), lambda b, s: (b, s, 0))
    return pl.pallas_call(
        functools.partial(_outproj_kernel, tm=tm),
        grid=(B, S // tm),
        in_specs=[row(D), row(D_RNN), row(D_ATTN), pl.BlockSpec((1, 6, D), lambda b, s: (b, 0, 0)),
                  const((1, D_ATTN)), const((D_RNN, D)), const((D_ATTN, D)), const((1, D)),
                  const((D, LANES)), const((D, LANES)), const((1, LANES))],
        out_specs=[row(D), row(D), row(LANES), const((1, LANES))],
        out_shape=[jax.ShapeDtypeStruct((B, S, D), F32), jax.ShapeDtypeStruct((B, S, D), BF16),
                   jax.ShapeDtypeStruct((B, S, LANES), F32), jax.ShapeDtypeStruct((1, LANES), F32)],
        scratch_shapes=[pltpu.VMEM((1, LANES), F32)],
        compiler_params=_cparams("arbitrary", "arbitrary"),
        name="outproj",
    )(x, y_rnn, y_attn, mod6, attn_out_g.reshape(1, D_ATTN), wor, woa, norm2_g.reshape(1, D),
      wr_hi, wr_lo, br)


GLU_GROUP = 2 * LANES


def _degroup_kernel(w_ref, p_ref, o_ref):
    p = p_ref[...]
    for g in range(2 * D_FF // GLU_GROUP):
        cols = slice(g * GLU_GROUP, (g + 1) * GLU_GROUP)
        o_ref[0, :, cols] = _dot(w_ref[0, :, cols].astype(BF16), p).astype(BF16)


def _degroup_w1(w1):
    E, D, F2 = w1.shape
    src = jnp.arange(GLU_GROUP)
    dst = jnp.where(src % 2 == 0, src // 2, LANES + src // 2)
    perm = (dst[:, None] == jnp.arange(GLU_GROUP)[None, :]).astype(BF16)
    tr = 512
    return pl.pallas_call(
        _degroup_kernel,
        grid=(E, D // tr),
        in_specs=[pl.BlockSpec((1, tr, F2), lambda e, r: (e, r, 0)),
                  pl.BlockSpec((GLU_GROUP, GLU_GROUP), lambda e, r: (0, 0))],
        out_specs=pl.BlockSpec((1, tr, F2), lambda e, r: (e, r, 0)),
        out_shape=jax.ShapeDtypeStruct((E, D, F2), BF16),
        compiler_params=_cparams("arbitrary", "arbitrary"),
        name="degroup_w1",
    )(w1, perm)


def _expert_kernel(be_ref, nb_ref, x_ref, w1_ref, b1_ref, w2_ref, b2_ref, y_ref):
    del be_ref

    @pl.when(pl.program_id(0) < nb_ref[0])
    def _():
        u = _dot(x_ref[...], w1_ref[0]) + b1_ref[0]
        acts = []
        for g in range(2 * D_FF // GLU_GROUP):
            ug = jnp.minimum(u[:, g * GLU_GROUP:g * GLU_GROUP + LANES], SWIGLU_LIMIT)
            ul = jnp.clip(u[:, g * GLU_GROUP + LANES:(g + 1) * GLU_GROUP], -SWIGLU_LIMIT, SWIGLU_LIMIT)
            acts.append((ug * jax.nn.sigmoid(SWIGLU_ALPHA * ug) * (ul + 1.0)).astype(BF16))
        act = jnp.concatenate(acts, axis=1)
        y_ref[...] = (_dot(act, w2_ref[0]) + b2_ref[0]).astype(y_ref.dtype)

    @pl.when(pl.program_id(0) >= nb_ref[0])
    def _():
        y_ref[...] = jnp.zeros(y_ref.shape, y_ref.dtype)


def _experts(x_sorted, block_e, n_used, w1, b1, w2, b2, mb):
    n_rows, D = x_sorted.shape
    E = w1.shape[0]
    w1p = _degroup_w1(w1)
    ng = 2 * D_FF // GLU_GROUP
    b1p = jnp.swapaxes(b1.reshape(E, ng, LANES, 2), 2, 3).reshape(E, 1, 2 * D_FF)
    wspec = lambda k, n: pl.BlockSpec((1, k, n), lambda i, be, nb: (be[i], 0, 0))
    grid_spec = pltpu.PrefetchScalarGridSpec(
        num_scalar_prefetch=2,
        grid=(n_rows // mb,),
        in_specs=[pl.BlockSpec((mb, D), lambda i, be, nb: (jnp.minimum(i, nb[0] - 1), 0)),
                  wspec(D, 2 * D_FF), wspec(1, 2 * D_FF), wspec(D_FF, D), wspec(1, D)],
        out_specs=pl.BlockSpec((mb, D), lambda i, be, nb: (i, 0)),
    )
    return pl.pallas_call(
        _expert_kernel,
        grid_spec=grid_spec,
        out_shape=jax.ShapeDtypeStruct((n_rows, D), BF16),
        compiler_params=_cparams("arbitrary"),
        name="experts",
    )(block_e, n_used, x_sorted, w1p, b1p, w2.astype(BF16), b2.reshape(E, 1, D))


def _combine_kernel(x1_ref, yg_ref, meta_ref, mod_ref, o_ref):
    meta = meta_ref[0]
    ff = meta[:, TOPK_EXPERTS:TOPK_EXPERTS + 1] * yg_ref[0].astype(F32)
    for jj in range(1, TOPK_EXPERTS):
        ff = ff + meta[:, TOPK_EXPERTS + jj:TOPK_EXPERTS + jj + 1] * yg_ref[jj].astype(F32)
    o_ref[0] = x1_ref[0] + mod_ref[0][5:6] * ff


def _combine(x1, yg, meta, mod6, tm):
    B, S, D = x1.shape
    nt = S // tm
    return pl.pallas_call(
        _combine_kernel,
        grid=(B, nt),
        in_specs=[pl.BlockSpec((1, tm, D), lambda b, s: (b, s, 0)),
                  pl.BlockSpec((TOPK_EXPERTS, tm, D), lambda b, s: (0, b * nt + s, 0)),
                  pl.BlockSpec((1, tm, LANES), lambda b, s: (b, s, 0)),
                  pl.BlockSpec((1, 6, D), lambda b, s: (b, 0, 0))],
        out_specs=pl.BlockSpec((1, tm, D), lambda b, s: (b, s, 0)),
        out_shape=jax.ShapeDtypeStruct((B, S, D), F32),
        compiler_params=_cparams("arbitrary", "arbitrary"),
        name="combine",
    )(x1, yg, meta, mod6)


def _tile(n, pref):
    t = pref
    while n % t:
        t //= 2
    return t


def _layer(x, c, w_ada, b_ada, norm1_g, w_in, conv_w, conv_b, w_rg_a, b_rg_a, w_rg_x, b_rg_x,
           lru_lambda, q_norm_g, k_norm_g, kidx_norm_g, rg_out_g, attn_out_g, w_out, norm2_g,
           w_router, b_router, w1, b1, w2, b2):
    B, S, D = x.shape
    T = B * S
    tm = _tile(S, 512)
    mod6 = _ada(c, w_ada, b_ada).reshape(B, 6, D)

    xr, xg, q, k, v, qi, kiw = _inproj(x, mod6, norm1_g, w_in, q_norm_g, k_norm_g, kidx_norm_g, tm)
    y_rnn = _rglru(xr, xg, conv_w, conv_b, w_rg_a, b_rg_a, w_rg_x, b_rg_x, lru_lambda, rg_out_g,
                   _tile(S, 256))

    ki = kiw[..., :IDX_DIM]
    ki_hi, ki_lo = _split(ki)
    ki_ext = jnp.concatenate([ki_hi, ki_hi, ki_lo, jnp.zeros_like(ki_hi)], axis=-1)
    qi_h = jnp.swapaxes(qi.reshape(B, S, IDX_HEADS, IDX_DIM), 1, 2)
    qi_hi, qi_lo = _split(qi_h)
    qi_ext = jnp.concatenate([qi_hi, qi_lo, qi_hi, jnp.zeros_like(qi_hi)], axis=-1)
    w_t = jnp.swapaxes(kiw[..., IDX_DIM:IDX_DIM + IDX_HEADS], 1, 2)
    v_h = v.reshape(B, S, N_HEADS, HEAD_DIM)
    v_ext = jnp.concatenate([v_h, jnp.ones((B, S, N_HEADS, 1), BF16),
                             jnp.zeros((B, S, N_HEADS, V_ROWS - HEAD_DIM - 1), BF16)], axis=-1)
    vt = jnp.swapaxes(v_ext.reshape(B, S, N_HEADS * V_ROWS), 1, 2)
    qb = _tile(S, 256)
    y_attn_t = _dsa(ki_ext, k, vt, qi_ext, q, w_t, qb, _tile(S, 512))
    y_attn = jnp.swapaxes(y_attn_t, 1, 2)

    x1, h2, meta, cnt = _outproj(x, y_rnn, y_attn, mod6, attn_out_g, w_out, norm2_g,
                                 w_router, b_router, tm)

    mb = 512
    top_e = meta[..., 0:TOPK_EXPERTS].astype(jnp.int32).reshape(T, TOPK_EXPERTS)
    rank = meta[..., 2 * TOPK_EXPERTS:3 * TOPK_EXPERTS].astype(jnp.int32).reshape(T, TOPK_EXPERTS)
    counts = cnt[0, :N_EXPERTS].astype(jnp.int32)
    padded = (counts + mb - 1) // mb * mb
    pad_end = jnp.cumsum(padded)
    pad_start = pad_end - padded
    dest = pad_start[top_e] + rank
    n_rows = T * TOPK_EXPERTS + N_EXPERTS * mb
    n_blocks = n_rows // mb
    block_start = jnp.arange(n_blocks, dtype=jnp.int32) * mb
    block_e = jnp.minimum(jnp.sum((pad_end[None, :] <= block_start[:, None]).astype(jnp.int32), axis=1),
                          N_EXPERTS - 1)
    tok = jnp.broadcast_to(jnp.arange(T, dtype=jnp.int32)[:, None], (T, TOPK_EXPERTS))
    row_tok = jnp.zeros((n_rows,), jnp.int32).at[dest.reshape(-1)].set(tok.reshape(-1))
    x_sorted = h2.reshape(T, D)[row_tok]
    n_used = (pad_end[N_EXPERTS - 1:] // mb).astype(jnp.int32)
    y_sorted = _experts(x_sorted, block_e, n_used, w1, b1, w2, b2, mb)
    yg = y_sorted[dest.T]
    return _combine(x1, yg, meta, mod6, tm)


def kernel(x, c, w_ada, b_ada, norm1_g, w_in, conv_w, conv_b, w_rg_a, b_rg_a, w_rg_x, b_rg_x,
           lru_lambda, q_norm_g, k_norm_g, kidx_norm_g, rg_out_g, attn_out_g, w_out, norm2_g,
           w_router, b_router, w1, b1, w2, b2):
    depth = w_ada.shape[0]
    for l in range(depth):
        x = _layer(x, c, w_ada[l], b_ada[l], norm1_g[l], w_in[l], conv_w[l], conv_b[l],
                   w_rg_a[l], b_rg_a[l], w_rg_x[l], b_rg_x[l], lru_lambda[l], q_norm_g[l],
                   k_norm_g[l], kidx_norm_g[l], rg_out_g[l], attn_out_g[l], w_out[l], norm2_g[l],
                   w_router[l], b_router[l], w1[l], b1[l], w2[l], b2[l])
    return x
```

```python
import functools

import jax
import jax.numpy as jnp
from jax import lax
from jax.experimental import pallas as pl
from jax.experimental.pallas import tpu as pltpu

D_MODEL = 1024
CHUNK = 64
CHUNK_SHIFT = 6
D_RNN = 512
RNN_BLOCKS = 8
RNN_BLOCK = 64
CONV_WIDTH = 4
RG_C = 8.0
N_HEADS = 8
HEAD_DIM = 64
D_ATTN = 512
IDX_HEADS = 8
IDX_DIM = 64
TOPK_KEYS_MAX = 256
N_EXPERTS = 32
TOPK_EXPERTS = 4
D_FF = 1024
SWIGLU_ALPHA = 1.702
SWIGLU_LIMIT = 7.0
EPS = 1e-6

LANES = 128
SUBLANES = 8
TINY_F32 = 1.1754943508222875e-38
REDUCE_CHAINS = 8
IDX_K = 256
NEG_BIG = -(2.0 ** 100)
LOG2E = 1.4426950408889634
V_ROWS = 80
VMEM_LIMIT = 56 * 1024 * 1024

F32 = jnp.float32
BF16 = jnp.bfloat16


def _dot(a, b):
    return jnp.dot(a, b, preferred_element_type=F32)


def _dot_nt(a, b):
    return lax.dot_general(a, b, (((1,), (1,)), ((), ())), preferred_element_type=F32)


def _split(a):
    hi = a.astype(BF16)
    lo = (a - hi.astype(F32)).astype(BF16)
    return hi, lo


def _dot3(a_hi, a_lo, b_hi, b_lo):
    return _dot(a_hi, b_hi) + (_dot(a_hi, b_lo) + _dot(a_lo, b_hi))


def _rms(x, g):
    ms = jnp.mean(x * x, axis=-1, keepdims=True)
    return x * lax.rsqrt(ms + EPS) * g


def _cparams(*sem):
    return pltpu.CompilerParams(dimension_semantics=sem, vmem_limit_bytes=VMEM_LIMIT)


def _ada_kernel(c_ref, w_ref, b_ref, o_ref):
    c = c_ref[...]
    s = c * jax.nn.sigmoid(c)
    s_hi, s_lo = _split(s)
    w_hi, w_lo = _split(w_ref[...])
    o_ref[...] = _dot3(s_hi, s_lo, w_hi, w_lo) + b_ref[...]


def _ada(c, w_ada, b_ada):
    B, D = c.shape
    N = w_ada.shape[1]
    rows = 16
    c_pad = jnp.zeros((rows, D), F32).at[:B].set(c)
    tn = 1024
    out = pl.pallas_call(
        _ada_kernel,
        grid=(N // tn,),
        in_specs=[pl.BlockSpec((rows, D), lambda j: (0, 0)),
                  pl.BlockSpec((D, tn), lambda j: (0, j)),
                  pl.BlockSpec((1, tn), lambda j: (0, j))],
        out_specs=pl.BlockSpec((rows, tn), lambda j: (0, j)),
        out_shape=jax.ShapeDtypeStruct((rows, N), F32),
        compiler_params=_cparams("arbitrary"),
        name="ada",
    )(c_pad, w_ada, b_ada.reshape(1, N))
    return out[:B]


def _inproj_kernel(x_ref, mod_ref, g1_ref, wmain_ref, widx_hi_ref, widx_lo_ref, bd_ref,
                   qg_ref, kg_ref, kig_ref,
                   xr_ref, xg_ref, q_ref, k_ref, v_ref, qi_ref, kiw_ref):
    x = x_ref[0]
    mod = mod_ref[0]
    h = _rms(x, g1_ref[...]) * (1.0 + mod[1:2]) + mod[0:1]
    h_hi, h_lo = _split(h)
    bd = bd_ref[...]

    def head_norm(z, g):
        sq_hi, sq_lo = _split(z * z)
        ms = _dot(sq_hi, bd) + _dot(sq_lo, bd)
        return z * lax.rsqrt(ms + EPS) * g

    xr_ref[0] = _dot(h_hi, wmain_ref[:, 0:512])
    xg_ref[0] = _dot(h_hi, wmain_ref[:, 512:1024])
    zq = _dot(h_hi, wmain_ref[:, 1024:1536])
    q_ref[0] = (head_norm(zq, qg_ref[...]) * (HEAD_DIM ** -0.5 * LOG2E)).astype(BF16)
    zk = _dot(h_hi, wmain_ref[:, 1536:2048])
    k_ref[0] = head_norm(zk, kg_ref[...]).astype(BF16)
    v_ref[0] = _dot(h_hi, wmain_ref[:, 2048:2560]).astype(BF16)

    zi = _dot3(h_hi, h_lo, widx_hi_ref[...], widx_lo_ref[...])
    qi_ref[0] = zi[:, 0:512] * (IDX_DIM ** -0.5)
    kw = zi[:, 512:640]
    lane = lax.broadcasted_iota(jnp.int32, kw.shape, 1)
    is_ki = lane < IDX_DIM
    ms = jnp.sum(jnp.where(is_ki, kw * kw, 0.0), axis=-1, keepdims=True) * (1.0 / IDX_DIM)
    kiw_ref[0] = jnp.where(is_ki, kw * lax.rsqrt(ms + EPS) * kig_ref[...], kw * (IDX_HEADS ** -0.5))


def _inproj(x, mod6, norm1_g, w_in, q_norm_g, k_norm_g, kidx_norm_g, tm):
    B, S, D = x.shape
    wmain = w_in[:, :2560].astype(BF16)
    widx = jnp.concatenate([w_in[:, 2560:3144], jnp.zeros((D, 640 - 584), F32)], axis=1)
    widx_hi = widx.astype(BF16)
    widx_lo = (widx - widx_hi.astype(F32)).astype(BF16)
    head = jnp.arange(D_ATTN) // HEAD_DIM
    bd = jnp.where(head[:, None] == head[None, :], 1.0 / HEAD_DIM, 0.0).astype(BF16)
    qg = jnp.tile(q_norm_g, N_HEADS).reshape(1, D_ATTN)
    kg = jnp.tile(k_norm_g, N_HEADS).reshape(1, D_ATTN)
    kig = jnp.concatenate([kidx_norm_g, jnp.zeros((LANES - IDX_DIM,), F32)]).reshape(1, LANES)
    const = lambda shape: pl.BlockSpec(shape, lambda b, s: (0,) * len(shape))
    row = lambda n: pl.BlockSpec((1, tm, n), lambda b, s: (b, s, 0))
    return pl.pallas_call(
        _inproj_kernel,
        grid=(B, S // tm),
        in_specs=[row(D), pl.BlockSpec((1, 6, D), lambda b, s: (b, 0, 0)), const((1, D)),
                  const((D, 2560)), const((D, 640)), const((D, 640)), const((D_ATTN, D_ATTN)),
                  const((1, D_ATTN)), const((1, D_ATTN)), const((1, LANES))],
        out_specs=[row(512), row(512), row(512), row(512), row(512), row(512), row(LANES)],
        out_shape=[jax.ShapeDtypeStruct((B, S, 512), F32), jax.ShapeDtypeStruct((B, S, 512), F32),
                   jax.ShapeDtypeStruct((B, S, 512), BF16), jax.ShapeDtypeStruct((B, S, 512), BF16),
                   jax.ShapeDtypeStruct((B, S, 512), BF16), jax.ShapeDtypeStruct((B, S, 512), F32),
                   jax.ShapeDtypeStruct((B, S, LANES), F32)],
        compiler_params=_cparams("arbitrary", "arbitrary"),
        name="inproj",
    )(x, mod6, norm1_g.reshape(1, D), wmain, widx_hi, widx_lo, bd, qg, kg, kig)


def _rglru_kernel(xr_ref, xg_ref, cw_ref, cb_ref, wa_hi_ref, wa_lo_ref, wx_hi_ref, wx_lo_ref,
                  ba_ref, bx_ref, lam_ref, g_ref, y_ref, ext_ref, hlast_ref, *, ts):
    @pl.when(pl.program_id(1) == 0)
    def _():
        ext_ref[0:8, :] = jnp.zeros((8, D_RNN), F32)
        hlast_ref[...] = jnp.zeros((1, D_RNN), F32)

    xr = xr_ref[0]
    ext_ref[8:8 + ts, :] = xr
    cw = cw_ref[...]
    xc = cb_ref[...] + cw[0:1] * ext_ref[5:5 + ts, :]
    for j in range(1, CONV_WIDTH):
        xc = xc + cw[j:j + 1] * ext_ref[5 + j:5 + j + ts, :]
    ext_ref[0:8, :] = xr[ts - 8:ts]

    xc_hi, xc_lo = _split(xc)
    r = jax.nn.sigmoid(_dot3(xc_hi, xc_lo, wa_hi_ref[...], wa_lo_ref[...]) + ba_ref[...])
    i = jax.nn.sigmoid(_dot3(xc_hi, xc_lo, wx_hi_ref[...], wx_lo_ref[...]) + bx_ref[...])
    lam = lam_ref[...]
    sp = jnp.maximum(-lam, 0.0) + jnp.log1p(jnp.exp(-jnp.abs(lam)))
    log_a = (-RG_C) * r * sp
    a = jnp.exp(log_a)
    mult = jnp.sqrt(-jnp.tanh(log_a) * (a * a + 1.0))
    u = mult * i * xc

    row = lax.broadcasted_iota(jnp.int32, (ts, D_RNN), 0)
    acc_a, acc_h = a, u
    d = 1
    while d < ts:
        sh_a = pltpu.roll(acc_a, d, 0)
        sh_h = pltpu.roll(acc_h, d, 0)
        ok = row >= d
        acc_h = jnp.where(ok, acc_a * sh_h + acc_h, acc_h)
        acc_a = jnp.where(ok, acc_a * sh_a, acc_a)
        d *= 2
    hseq = acc_h + acc_a * hlast_ref[...]
    hlast_ref[...] = hseq[ts - 1:ts]

    xg = xg_ref[0]
    gelu = 0.5 * xg * (1.0 + jnp.tanh(0.7978845608028654 * (xg + 0.044715 * (xg * xg * xg))))
    y_ref[0] = _rms(gelu * hseq, g_ref[...]).astype(BF16)


def _block_diag(w):
    H, n, _ = w.shape
    eye = jnp.eye(H, dtype=w.dtype)
    return (eye[:, None, :, None] * w[:, :, None, :]).reshape(H * n, H * n)


def _rglru(xr, xg, conv_w, conv_b, w_a, b_a, w_x, b_x, lam, g, ts):
    B, S, _ = xr.shape
    wa_hi, wa_lo = _split(_block_diag(w_a))
    wx_hi, wx_lo = _split(_block_diag(w_x))
    const = lambda shape: pl.BlockSpec(shape, lambda b, s: (0,) * len(shape))
    row = pl.BlockSpec((1, ts, D_RNN), lambda b, s: (b, s, 0))
    vec = lambda a: a.reshape(1, D_RNN)
    return pl.pallas_call(
        functools.partial(_rglru_kernel, ts=ts),
        grid=(B, S // ts),
        in_specs=[row, row, const((CONV_WIDTH, D_RNN)), const((1, D_RNN)),
                  const((D_RNN, D_RNN)), const((D_RNN, D_RNN)), const((D_RNN, D_RNN)), const((D_RNN, D_RNN)),
                  const((1, D_RNN)), const((1, D_RNN)), const((1, D_RNN)), const((1, D_RNN))],
        out_specs=row,
        out_shape=jax.ShapeDtypeStruct((B, S, D_RNN), BF16),
        scratch_shapes=[pltpu.VMEM((ts + 8, D_RNN), F32), pltpu.VMEM((1, D_RNN), F32)],
        compiler_params=_cparams("arbitrary", "arbitrary"),
        name="rglru",
    )(xr, xg, conv_w, vec(conv_b), wa_hi, wa_lo, wx_hi, wx_lo, vec(b_a), vec(b_x), vec(lam), vec(g))


def _dsa_kernel(ki_ref, k_ref, vt_ref, qi_ref, q_ref, w_ref, y_ref,
                sc_ref, qh_ref, acc_ref, lga_ref, lgb_ref, *, qb, tk, n_sel, seq):
    j = pl.program_id(1)
    n_kt = ((j + 1) * qb + tk - 1) // tk
    q_idx = j * qb + lax.broadcasted_iota(jnp.int32, (1, qb), 1)
    q_chunk = jnp.right_shift(q_idx, CHUNK_SHIFT)

    def key_index(kt):
        return kt * tk + lax.broadcasted_iota(jnp.int32, (tk, 1), 0)

    w = w_ref[0]

    def fold8(a):
        return a.reshape(tk // SUBLANES, SUBLANES, qb)

    def count8(off, pred):
        rows = tk // REDUCE_CHAINS
        parts = []
        for i in range(REDUCE_CHAINS):
            blk = sc_ref[pl.ds(off + i * rows, rows), :]
            ind = jnp.where(pred(blk, i * rows), 1.0, 0.0)
            parts.append(jnp.sum(ind.reshape(rows // SUBLANES, SUBLANES, qb), axis=0))
        while len(parts) > 1:
            parts = [parts[i] + parts[i + 1] for i in range(0, len(parts), 2)]
        return parts[0]

    def score_tile(kt, carry):
        smin, smax = carry
        off = pl.multiple_of(kt * tk, tk)
        ki = ki_ref[0, pl.ds(off, tk), :]
        sc = jnp.zeros((tk, qb), F32)
        for h in range(IDX_HEADS):
            s = _dot_nt(ki, qi_ref[0, h])
            sc = sc + w[h:h + 1] * jnp.maximum(s, 0.0)
        adm = jnp.right_shift(key_index(kt), CHUNK_SHIFT) <= q_chunk
        sc_ref[pl.ds(off, tk), :] = jnp.where(adm, sc, -jnp.inf)
        smin = jnp.minimum(smin, jnp.min(fold8(jnp.where(adm, sc, jnp.inf)), axis=0))
        smax = jnp.maximum(smax, jnp.max(fold8(jnp.where(adm, sc, -jnp.inf)), axis=0))
        return smin, smax

    smin8, smax8 = lax.fori_loop(0, n_kt, score_tile,
                                 (jnp.full((SUBLANES, qb), jnp.inf, F32),
                                  jnp.full((SUBLANES, qb), -jnp.inf, F32)))
    smin = jnp.min(smin8, axis=0, keepdims=True)
    smax = jnp.max(smax8, axis=0, keepdims=True)

    def count_ge(t):
        def body(kt, c8):
            off = pl.multiple_of(kt * tk, tk)
            return c8 + count8(off, lambda blk, r0: blk >= t)
        c8 = lax.fori_loop(0, n_kt, body, jnp.zeros((SUBLANES, qb), F32))
        return jnp.sum(c8, axis=0, keepdims=True)

    n_adm = (q_chunk + 1) * CHUNK
    take_all = n_adm <= n_sel
    nf = float(n_sel)

    def probe_step(st, t, may_converge):
        lo, hi, cnt_hi, thr, exact, done_f = st
        done = done_f > 0.0
        c = count_ge(t)
        inside = jnp.logical_and(t > lo, t < hi)
        conv = jnp.logical_and(may_converge, jnp.logical_not(inside))
        hit = jnp.logical_and(c == nf, jnp.logical_not(conv))
        ge = c >= nf
        new_done = jnp.logical_or(done, jnp.logical_or(hit, conv))
        thr_n = jnp.where(done, thr, jnp.where(hit, t, jnp.where(conv, lo, thr)))
        exact_n = jnp.where(done, exact, jnp.where(hit, 1.0, exact))
        upd = jnp.logical_not(new_done)
        move_lo = jnp.logical_and(upd, ge)
        move_hi = jnp.logical_and(upd, jnp.logical_not(ge))
        return (jnp.where(move_lo, t, lo), jnp.where(move_hi, t, hi), jnp.where(move_hi, c, cnt_hi),
                thr_n, exact_n, jnp.where(new_done, 1.0, 0.0))

    def midpoint(lo, hi):
        return lo + (hi - lo) * 0.5

    zeros = jnp.zeros((1, qb), F32)
    take_all_f = jnp.where(take_all, 1.0, 0.0)
    st = (smin, smax, zeros, jnp.full((1, qb), -jnp.inf, F32), take_all_f, take_all_f)
    lo, hi, cnt_hi, thr, exact, done_f = st
    c_top = count_ge(hi)
    top_hit = c_top == nf
    top_ge = c_top >= nf
    live = done_f == 0.0
    st = (jnp.where(jnp.logical_and(live, top_ge), hi, lo), hi,
          jnp.where(jnp.logical_and(live, jnp.logical_not(top_ge)), c_top, cnt_hi),
          jnp.where(jnp.logical_and(live, top_hit), hi, thr),
          jnp.where(jnp.logical_and(live, top_hit), 1.0, exact),
          jnp.where(jnp.logical_and(live, top_hit), 1.0, done_f))
    for special in (0.0, TINY_F32):
        lo, hi = st[0], st[1]
        sp = jnp.full((1, qb), special, F32)
        t = jnp.where(jnp.logical_and(sp > lo, sp < hi), sp, midpoint(lo, hi))
        st = probe_step(st, t, True)

    def bis_cond(st):
        return jnp.min(st[5]) < 1.0

    def bis_body(st):
        return probe_step(st, midpoint(st[0], st[1]), True)

    lo, hi, cnt_hi, thr, exact, _ = lax.while_loop(bis_cond, bis_body, st)
    need = nf - cnt_hi

    def count_tie(jcut):
        def body(kt, c8):
            off = pl.multiple_of(kt * tk, tk)

            def tied_before_cut(blk, r0):
                key = off + r0 + lax.broadcasted_iota(jnp.int32, (blk.shape[0], 1), 0)
                return jnp.logical_and(blk == thr, key <= jcut)
            return c8 + count8(off, tied_before_cut)
        c8 = lax.fori_loop(0, n_kt, body, jnp.zeros((SUBLANES, qb), F32))
        return jnp.sum(c8, axis=0, keepdims=True)

    def tie_cond(st):
        jlo, jhi = st
        return jnp.max(jnp.where(jnp.logical_and(exact == 0.0, jlo < jhi), 1.0, 0.0)) > 0.0

    def tie_body(st):
        jlo, jhi = st
        jmid = jnp.right_shift(jlo + jhi, 1)
        ok = count_tie(jmid) >= need
        return jnp.where(ok, jlo, jmid + 1), jnp.where(ok, jmid, jhi)

    jlo, _ = lax.while_loop(tie_cond, tie_body,
                            (jnp.zeros((1, qb), jnp.int32), jnp.full((1, qb), seq - 1, jnp.int32)))
    jcut = jnp.where(take_all, -1, jnp.where(exact > 0.0, seq, jlo))

    acc_ref[...] = jnp.zeros((N_HEADS * V_ROWS, qb), F32)
    lane = lax.broadcasted_iota(jnp.int32, (qb, LANES), 1)
    for p in range(N_HEADS // 2):
        qp = q_ref[0, :, p * LANES:(p + 1) * LANES]
        qh_ref[2 * p] = jnp.where(lane < HEAD_DIM, qp, jnp.zeros_like(qp))
        qh_ref[2 * p + 1] = jnp.where(lane >= HEAD_DIM, qp, jnp.zeros_like(qp))

    def masked_logits(kt, dst_ref):
        kt = jnp.minimum(kt, n_kt - 1)
        off = pl.multiple_of(kt * tk, tk)
        blk = sc_ref[pl.ds(off, tk), :]
        sel = jnp.logical_or(blk > thr, jnp.logical_and(blk == thr, key_index(kt) <= jcut))
        bias = jnp.where(sel, 0.0, NEG_BIG).astype(BF16)
        for h in range(N_HEADS):
            p = h // 2
            kp = k_ref[0, pl.ds(off, tk), p * LANES:(p + 1) * LANES]
            dst_ref[h] = _dot_nt(kp, qh_ref[h]).astype(BF16) + bias

    def softmax_pv(kt, src_ref, ms):
        off = pl.multiple_of(kt * tk, tk)
        m_new = []
        for h in range(N_HEADS):
            logits = src_ref[h]
            m_h = jnp.maximum(ms[h], jnp.max(logits, axis=0, keepdims=True).astype(F32))
            alpha = jnp.exp2(ms[h] - m_h)
            probs = jnp.exp2(logits - m_h.astype(BF16))
            vt = vt_ref[0, h * V_ROWS:(h + 1) * V_ROWS, pl.ds(off, tk)]
            rows = slice(h * V_ROWS, (h + 1) * V_ROWS)
            acc_ref[rows, :] = alpha * acc_ref[rows, :] + _dot(vt, probs)
            m_new.append(m_h)
        return tuple(m_new)

    masked_logits(0, lga_ref)

    def tile_pair(i, ms):
        kt0 = 2 * i
        masked_logits(kt0 + 1, lgb_ref)
        ms = softmax_pv(kt0, lga_ref, ms)

        def second(ms):
            masked_logits(kt0 + 2, lga_ref)
            return softmax_pv(kt0 + 1, lgb_ref, ms)

        return lax.cond(kt0 + 1 < n_kt, second, lambda ms: ms, ms)

    m0 = tuple(jnp.full((1, qb), NEG_BIG, F32) for _ in range(N_HEADS))
    lax.fori_loop(0, (n_kt + 1) // 2, tile_pair, m0)
    for h in range(N_HEADS):
        num = acc_ref[h * V_ROWS:h * V_ROWS + HEAD_DIM, :]
        den = acc_ref[h * V_ROWS + HEAD_DIM:h * V_ROWS + HEAD_DIM + 1, :]
        y_ref[0, h * HEAD_DIM:(h + 1) * HEAD_DIM, :] = num / den


def _dsa(ki_ext, k, vt, qi_ext, q, w_t, qb, tk):
    B, S, _ = k.shape
    n_sel = min(TOPK_KEYS_MAX, S // 4)
    resident = lambda shape: pl.BlockSpec(shape, lambda b, j: (b, 0, 0), pipeline_mode=pl.Buffered(1))
    return pl.pallas_call(
        functools.partial(_dsa_kernel, qb=qb, tk=tk, n_sel=n_sel, seq=S),
        grid=(B, S // qb),
        in_specs=[resident((1, S, IDX_K)), resident((1, S, D_ATTN)), resident((1, N_HEADS * V_ROWS, S)),
                  pl.BlockSpec((1, IDX_HEADS, qb, IDX_K), lambda b, j: (b, 0, j, 0)),
                  pl.BlockSpec((1, qb, D_ATTN), lambda b, j: (b, j, 0)),
                  pl.BlockSpec((1, IDX_HEADS, qb), lambda b, j: (b, 0, j))],
        out_specs=pl.BlockSpec((1, D_ATTN, qb), lambda b, j: (b, 0, j)),
        out_shape=jax.ShapeDtypeStruct((B, D_ATTN, S), F32),
        scratch_shapes=[pltpu.VMEM((S, qb), F32), pltpu.VMEM((N_HEADS, qb, LANES), BF16),
                        pltpu.VMEM((N_HEADS * V_ROWS, qb), F32),
                        pltpu.VMEM((N_HEADS, tk, qb), BF16), pltpu.VMEM((N_HEADS, tk, qb), BF16)],
        compiler_params=_cparams("arbitrary", "arbitrary"),
        name="dsa",
    )(ki_ext, k, vt, qi_ext, q, w_t)


def _outproj_kernel(x_ref, yr_ref, ya_ref, mod_ref, ag_ref, wor_ref, woa_ref, n2g_ref,
                    wr_hi_ref, wr_lo_ref, br_ref,
                    x1_ref, h2_ref, meta_ref, cnt_ref, base_ref, *, tm):
    @pl.when(jnp.logical_and(pl.program_id(0) == 0, pl.program_id(1) == 0))
    def _():
        base_ref[...] = jnp.zeros((1, LANES), F32)

    mod = mod_ref[0]
    ya = _rms(ya_ref[0], ag_ref[...]).astype(BF16)
    mix = _dot(yr_ref[0], wor_ref[...]) + _dot(ya, woa_ref[...])
    x1 = x_ref[0] + mod[2:3] * mix
    x1_ref[0] = x1
    h2 = _rms(x1, n2g_ref[...]) * (1.0 + mod[4:5]) + mod[3:4]
    h2_ref[0] = h2.astype(BF16)
    h2_hi, h2_lo = _split(h2)
    logits = _dot3(h2_hi, h2_lo, wr_hi_ref[...], wr_lo_ref[...]) + br_ref[...]

    lane = lax.broadcasted_iota(jnp.int32, (tm, LANES), 1).astype(F32)
    work = logits
    vals, onehots, idxs = [], [], []
    for _ in range(TOPK_EXPERTS):
        mx = jnp.max(work, axis=-1, keepdims=True)
        idx = jnp.min(jnp.where(work == mx, lane, float(LANES)), axis=-1, keepdims=True)
        oh = lane == idx
        vals.append(mx)
        idxs.append(idx)
        onehots.append(oh)
        work = jnp.where(oh, -jnp.inf, work)
    es = [jnp.exp(v - vals[0]) for v in vals]
    denom = es[0] + es[1] + es[2] + es[3]
    assigned = jnp.zeros((tm, LANES), F32)
    for oh in onehots:
        assigned = assigned + jnp.where(oh, 1.0, 0.0)
    r_i = lax.broadcasted_iota(jnp.int32, (tm, tm), 0)
    c_i = lax.broadcasted_iota(jnp.int32, (tm, tm), 1)
    ltri = jnp.where(c_i < r_i, 1.0, 0.0).astype(BF16)
    prior = _dot(ltri, assigned.astype(BF16)) + base_ref[...]
    meta = jnp.zeros((tm, LANES), F32)
    for jj in range(TOPK_EXPERTS):
        rank = jnp.sum(jnp.where(onehots[jj], prior, 0.0), axis=-1, keepdims=True)
        meta = jnp.where(lane == jj, idxs[jj], meta)
        meta = jnp.where(lane == TOPK_EXPERTS + jj, es[jj] / denom, meta)
        meta = jnp.where(lane == 2 * TOPK_EXPERTS + jj, rank, meta)
    meta_ref[0] = meta
    base = base_ref[...] + jnp.sum(assigned, axis=0, keepdims=True)
    base_ref[...] = base
    cnt_ref[...] = base


def _outproj(x, y_rnn, y_attn, mod6, attn_out_g, w_out, norm2_g, w_router, b_router, tm):
    B, S, D = x.shape
    wor = w_out[:D_RNN].astype(BF16)
    woa = w_out[D_RNN:].astype(BF16)
    wr = jnp.zeros((D, LANES), F32).at[:, :N_EXPERTS].set(w_router)
    wr_hi, wr_lo = _split(wr)
    br = jnp.full((1, LANES), NEG_BIG, F32).at[0, :N_EXPERTS].set(b_router)
    const = lambda shape: pl.BlockSpec(shape, lambda b, s: (0,) * len(shape))
    row = lambda n: pl.BlockSpec((1, tm, n), lambda b, s: (b, s, 0))
    return pl.pallas_call(
        functools.partial(_outproj_kernel, tm=tm),
        grid=(B, S // tm),
        in_specs=[row(D), row(D_RNN), row(D_ATTN), pl.BlockSpec((1, 6, D), lambda b, s: (b, 0, 0)),
                  const((1, D_ATTN)), const((D_RNN, D)), const((D_ATTN, D)), const((1, D)),
                  const((D, LANES)), const((D, LANES)), const((1, LANES))],
        out_specs=[row(D), row(D), row(LANES), const((1, LANES))],
        out_shape=[jax.ShapeDtypeStruct((B, S, D), F32), jax.ShapeDtypeStruct((B, S, D), BF16),
                   jax.ShapeDtypeStruct((B, S, LANES), F32), jax.ShapeDtypeStruct((1, LANES), F32)],
        scratch_shapes=[pltpu.VMEM((1, LANES), F32)],
        compiler_params=_cparams("arbitrary", "arbitrary"),
        name="outproj",
    )(x, y_rnn, y_attn, mod6, attn_out_g.reshape(1, D_ATTN), wor, woa, norm2_g.reshape(1, D),
      wr_hi, wr_lo, br)


GLU_GROUP = 2 * LANES


def _degroup_kernel(w_ref, p_ref, o_ref):
    p = p_ref[...]
    for g in range(2 * D_FF // GLU_GROUP):
        cols = slice(g * GLU_GROUP, (g + 1) * GLU_GROUP)
        o_ref[0, :, cols] = _dot(w_ref[0, :, cols].astype(BF16), p).astype(BF16)


def _degroup_w1(w1):
    E, D, F2 = w1.shape
    src = jnp.arange(GLU_GROUP)
    dst = jnp.where(src % 2 == 0, src // 2, LANES + src // 2)
    perm = (dst[:, None] == jnp.arange(GLU_GROUP)[None, :]).astype(BF16)
    tr = 512
    return pl.pallas_call(
        _degroup_kernel,
        grid=(E, D // tr),
        in_specs=[pl.BlockSpec((1, tr, F2), lambda e, r: (e, r, 0)),
                  pl.BlockSpec((GLU_GROUP, GLU_GROUP), lambda e, r: (0, 0))],
        out_specs=pl.BlockSpec((1, tr, F2), lambda e, r: (e, r, 0)),
        out_shape=jax.ShapeDtypeStruct((E, D, F2), BF16),
        compiler_params=_cparams("arbitrary", "arbitrary"),
        name="degroup_w1",
    )(w1, perm)


def _expert_kernel(be_ref, nb_ref, x_ref, w1_ref, b1_ref, w2_ref, b2_ref, y_ref):
    del be_ref

    @pl.when(pl.program_id(0) < nb_ref[0])
    def _():
        u = _dot(x_ref[...], w1_ref[0]) + b1_ref[0]
        acts = []
        for g in range(2 * D_FF // GLU_GROUP):
            ug = jnp.minimum(u[:, g * GLU_GROUP:g * GLU_GROUP + LANES], SWIGLU_LIMIT)
            ul = jnp.clip(u[:, g * GLU_GROUP + LANES:(g + 1) * GLU_GROUP], -SWIGLU_LIMIT, SWIGLU_LIMIT)
            acts.append((ug * jax.nn.sigmoid(SWIGLU_ALPHA * ug) * (ul + 1.0)).astype(BF16))
        act = jnp.concatenate(acts, axis=1)
        y_ref[...] = (_dot(act, w2_ref[0]) + b2_ref[0]).astype(y_ref.dtype)

    @pl.when(pl.program_id(0) >= nb_ref[0])
    def _():
        y_ref[...] = jnp.zeros(y_ref.shape, y_ref.dtype)


def _experts(x_sorted, block_e, n_used, w1, b1, w2, b2, mb):
    n_rows, D = x_sorted.shape
    E = w1.shape[0]
    w1p = _degroup_w1(w1)
    ng = 2 * D_FF // GLU_GROUP
    b1p = jnp.swapaxes(b1.reshape(E, ng, LANES, 2), 2, 3).reshape(E, 1, 2 * D_FF)
    wspec = lambda k, n: pl.BlockSpec((1, k, n), lambda i, be, nb: (be[i], 0, 0))
    grid_spec = pltpu.PrefetchScalarGridSpec(
        num_scalar_prefetch=2,
        grid=(n_rows // mb,),
        in_specs=[pl.BlockSpec((mb, D), lambda i, be, nb: (jnp.minimum(i, nb[0] - 1), 0)),
                  wspec(D, 2 * D_FF), wspec(1, 2 * D_FF), wspec(D_FF, D), wspec(1, D)],
        out_specs=pl.BlockSpec((mb, D), lambda i, be, nb: (i, 0)),
    )
    return pl.pallas_call(
        _expert_kernel,
        grid_spec=grid_spec,
        out_shape=jax.ShapeDtypeStruct((n_rows, D), F32),
        compiler_params=_cparams("arbitrary"),
        name="experts",
    )(block_e, n_used, x_sorted, w1p, b1p, w2.astype(BF16), b2.reshape(E, 1, D))


def _combine_kernel(x1_ref, yg_ref, meta_ref, mod_ref, o_ref):
    meta = meta_ref[0]
    ff = meta[:, TOPK_EXPERTS:TOPK_EXPERTS + 1] * yg_ref[0].astype(F32)
    for jj in range(1, TOPK_EXPERTS):
        ff = ff + meta[:, TOPK_EXPERTS + jj:TOPK_EXPERTS + jj + 1] * yg_ref[jj].astype(F32)
    o_ref[0] = x1_ref[0] + mod_ref[0][5:6] * ff


def _combine(x1, yg, meta, mod6, tm):
    B, S, D = x1.shape
    nt = S // tm
    return pl.pallas_call(
        _combine_kernel,
        grid=(B, nt),
        in_specs=[pl.BlockSpec((1, tm, D), lambda b, s: (b, s, 0)),
                  pl.BlockSpec((TOPK_EXPERTS, tm, D), lambda b, s: (0, b * nt + s, 0)),
                  pl.BlockSpec((1, tm, LANES), lambda b, s: (b, s, 0)),
                  pl.BlockSpec((1, 6, D), lambda b, s: (b, 0, 0))],
        out_specs=pl.BlockSpec((1, tm, D), lambda b, s: (b, s, 0)),
        out_shape=jax.ShapeDtypeStruct((B, S, D), F32),
        compiler_params=_cparams("arbitrary", "arbitrary"),
        name="combine",
    )(x1, yg, meta, mod6)


def _tile(n, pref):
    t = pref
    while n % t:
        t //= 2
    return t


def _layer(x, c, w_ada, b_ada, norm1_g, w_in, conv_w, conv_b, w_rg_a, b_rg_a, w_rg_x, b_rg_x,
           lru_lambda, q_norm_g, k_norm_g, kidx_norm_g, rg_out_g, attn_out_g, w_out, norm2_g,
           w_router, b_router, w1, b1, w2, b2):
    B, S, D = x.shape
    T = B * S
    tm = _tile(S, 512)
    mod6 = _ada(c, w_ada, b_ada).reshape(B, 6, D)

    xr, xg, q, k, v, qi, kiw = _inproj(x, mod6, norm1_g, w_in, q_norm_g, k_norm_g, kidx_norm_g, tm)
    y_rnn = _rglru(xr, xg, conv_w, conv_b, w_rg_a, b_rg_a, w_rg_x, b_rg_x, lru_lambda, rg_out_g,
                   _tile(S, 256))

    ki = kiw[..., :IDX_DIM]
    ki_hi, ki_lo = _split(ki)
    ki_ext = jnp.concatenate([ki_hi, ki_hi, ki_lo, jnp.zeros_like(ki_hi)], axis=-1)
    qi_h = jnp.swapaxes(qi.reshape(B, S, IDX_HEADS, IDX_DIM), 1, 2)
    qi_hi, qi_lo = _split(qi_h)
    qi_ext = jnp.concatenate([qi_hi, qi_lo, qi_hi, jnp.zeros_like(qi_hi)], axis=-1)
    w_t = jnp.swapaxes(kiw[..., IDX_DIM:IDX_DIM + IDX_HEADS], 1, 2)
    v_h = v.reshape(B, S, N_HEADS, HEAD_DIM)
    v_ext = jnp.concatenate([v_h, jnp.ones((B, S, N_HEADS, 1), BF16),
                             jnp.zeros((B, S, N_HEADS, V_ROWS - HEAD_DIM - 1), BF16)], axis=-1)
    vt = jnp.swapaxes(v_ext.reshape(B, S, N_HEADS * V_ROWS), 1, 2)
    qb = _tile(S, 256)
    y_attn_t = _dsa(ki_ext, k, vt, qi_ext, q, w_t, qb, _tile(S, 512))
    y_attn = jnp.swapaxes(y_attn_t, 1, 2)

    x1, h2, meta, cnt = _outproj(x, y_rnn, y_attn, mod6, attn_out_g, w_out, norm2_g,
                                 w_router, b_router, tm)

    mb = 512
    top_e = meta[..., 0:TOPK_EXPERTS].astype(jnp.int32).reshape(T, TOPK_EXPERTS)
    rank = meta[..., 2 * TOPK_EXPERTS:3 * TOPK_EXPERTS].astype(jnp.int32).reshape(T, TOPK_EXPERTS)
    counts = cnt[0, :N_EXPERTS].astype(jnp.int32)
    padded = (counts + mb - 1) // mb * mb
    pad_end = jnp.cumsum(padded)
    pad_start = pad_end - padded
    dest = pad_start[top_e] + rank
    n_rows = T * TOPK_EXPERTS + N_EXPERTS * mb
    n_blocks = n_rows // mb
    block_start = jnp.arange(n_blocks, dtype=jnp.int32) * mb
    block_e = jnp.minimum(jnp.sum((pad_end[None, :] <= block_start[:, None]).astype(jnp.int32), axis=1),
                          N_EXPERTS - 1)
    tok = jnp.broadcast_to(jnp.arange(T, dtype=jnp.int32)[:, None], (T, TOPK_EXPERTS))
    row_tok = jnp.zeros((n_rows,), jnp.int32).at[dest.reshape(-1)].set(
        tok.reshape(-1), unique_indices=True, mode="promise_in_bounds")
    x_sorted = h2.reshape(T, D)[row_tok]
    n_used = (pad_end[N_EXPERTS - 1:] // mb).astype(jnp.int32)
    y_sorted = _experts(x_sorted, block_e, n_used, w1, b1, w2, b2, mb)
    yg = y_sorted[dest.T]
    return _combine(x1, yg, meta, mod6, tm)


def kernel(x, c, w_ada, b_ada, norm1_g, w_in, conv_w, conv_b, w_rg_a, b_rg_a, w_rg_x, b_rg_x,
           lru_lambda, q_norm_g, k_norm_g, kidx_norm_g, rg_out_g, attn_out_g, w_out, norm2_g,
           w_router, b_router, w1, b1, w2, b2):
    depth = w_ada.shape[0]
    for l in range(depth):
        x = _layer(x, c, w_ada[l], b_ada[l], norm1_g[l], w_in[l], conv_w[l], conv_b[l],
                   w_rg_a[l], b_rg_a[l], w_rg_x[l], b_rg_x[l], lru_lambda[l], q_norm_g[l],
                   k_norm_g[l], kidx_norm_g[l], rg_out_g[l], attn_out_g[l], w_out[l], norm2_g[l],
                   w_router[l], b_router[l], w1[l], b1[l], w2[l], b2[l])
    return x
```

```python
import functools

import jax
import jax.numpy as jnp
from jax import lax
from jax.experimental import pallas as pl
from jax.experimental.pallas import tpu as pltpu

D_MODEL = 1024
CHUNK = 64
CHUNK_SHIFT = 6
D_RNN = 512
RNN_BLOCKS = 8
RNN_BLOCK = 64
CONV_WIDTH = 4
RG_C = 8.0
N_HEADS = 8
HEAD_DIM = 64
D_ATTN = 512
IDX_HEADS = 8
IDX_DIM = 64
TOPK_KEYS_MAX = 256
N_EXPERTS = 32
TOPK_EXPERTS = 4
D_FF = 1024
SWIGLU_ALPHA = 1.702
SWIGLU_LIMIT = 7.0
EPS = 1e-6

LANES = 128
SUBLANES = 8
TINY_F32 = 1.1754943508222875e-38
REDUCE_CHAINS = 8
IDX_K = 256
NEG_BIG = -(2.0 ** 100)
LOG2E = 1.4426950408889634
V_ROWS = 80
VMEM_LIMIT = 56 * 1024 * 1024

F32 = jnp.float32
BF16 = jnp.bfloat16


def _dot(a, b):
    return jnp.dot(a, b, preferred_element_type=F32)


def _dot_nt(a, b):
    return lax.dot_general(a, b, (((1,), (1,)), ((), ())), preferred_element_type=F32)


def _split(a):
    hi = a.astype(BF16)
    lo = (a - hi.astype(F32)).astype(BF16)
    return hi, lo


def _dot3(a_hi, a_lo, b_hi, b_lo):
    return _dot(a_hi, b_hi) + (_dot(a_hi, b_lo) + _dot(a_lo, b_hi))


def _pack_bf16_pairs(a):
    n = a.shape[1] // 2
    hi = lax.bitcast_convert_type(a[:, :n].astype(BF16).astype(F32), jnp.int32)
    lo = lax.bitcast_convert_type(a[:, n:].astype(BF16).astype(F32), jnp.int32)
    return jnp.bitwise_or(hi, lax.shift_right_logical(lo, 16))


def _unpack_bf16_pairs(u):
    hi = lax.bitcast_convert_type(jnp.bitwise_and(u, -65536), F32)
    lo = lax.bitcast_convert_type(lax.shift_left(u, 16), F32)
    return jnp.concatenate([hi, lo], axis=1)


def _rms(x, g):
    ms = jnp.mean(x * x, axis=-1, keepdims=True)
    return x * lax.rsqrt(ms + EPS) * g


def _cparams(*sem):
    return pltpu.CompilerParams(dimension_semantics=sem, vmem_limit_bytes=VMEM_LIMIT)


def _ada_kernel(c_ref, w_ref, b_ref, o_ref):
    c = c_ref[...]
    s = c * jax.nn.sigmoid(c)
    s_hi, s_lo = _split(s)
    w_hi, w_lo = _split(w_ref[...])
    o_ref[...] = _dot3(s_hi, s_lo, w_hi, w_lo) + b_ref[...]


def _ada(c, w_ada, b_ada):
    B, D = c.shape
    N = w_ada.shape[1]
    rows = 16
    c_pad = jnp.zeros((rows, D), F32).at[:B].set(c)
    tn = 1024
    out = pl.pallas_call(
        _ada_kernel,
        grid=(N // tn,),
        in_specs=[pl.BlockSpec((rows, D), lambda j: (0, 0)),
                  pl.BlockSpec((D, tn), lambda j: (0, j)),
                  pl.BlockSpec((1, tn), lambda j: (0, j))],
        out_specs=pl.BlockSpec((rows, tn), lambda j: (0, j)),
        out_shape=jax.ShapeDtypeStruct((rows, N), F32),
        compiler_params=_cparams("arbitrary"),
        name="ada",
    )(c_pad, w_ada, b_ada.reshape(1, N))
    return out[:B]


def _inproj_kernel(x_ref, mod_ref, g1_ref, wmain_ref, widx_hi_ref, widx_lo_ref, bd_ref,
                   qg_ref, kg_ref, kig_ref,
                   xr_ref, xg_ref, q_ref, k_ref, v_ref, qi_ref, kiw_ref):
    x = x_ref[0]
    mod = mod_ref[0]
    h = _rms(x, g1_ref[...]) * (1.0 + mod[1:2]) + mod[0:1]
    h_hi, h_lo = _split(h)
    bd = bd_ref[...]

    def head_norm(z, g):
        sq_hi, sq_lo = _split(z * z)
        ms = _dot(sq_hi, bd) + _dot(sq_lo, bd)
        return z * lax.rsqrt(ms + EPS) * g

    xr_ref[0] = _dot(h_hi, wmain_ref[:, 0:512])
    xg_ref[0] = _dot(h_hi, wmain_ref[:, 512:1024])
    zq = _dot(h_hi, wmain_ref[:, 1024:1536])
    q_ref[0] = (head_norm(zq, qg_ref[...]) * (HEAD_DIM ** -0.5 * LOG2E)).astype(BF16)
    zk = _dot(h_hi, wmain_ref[:, 1536:2048])
    k_ref[0] = head_norm(zk, kg_ref[...]).astype(BF16)
    v_ref[0] = _dot(h_hi, wmain_ref[:, 2048:2560]).astype(BF16)

    zi = _dot3(h_hi, h_lo, widx_hi_ref[...], widx_lo_ref[...])
    qi_ref[0] = zi[:, 0:512] * (IDX_DIM ** -0.5)
    kw = zi[:, 512:640]
    lane = lax.broadcasted_iota(jnp.int32, kw.shape, 1)
    is_ki = lane < IDX_DIM
    ms = jnp.sum(jnp.where(is_ki, kw * kw, 0.0), axis=-1, keepdims=True) * (1.0 / IDX_DIM)
    kiw_ref[0] = jnp.where(is_ki, kw * lax.rsqrt(ms + EPS) * kig_ref[...], kw * (IDX_HEADS ** -0.5))


def _inproj(x, mod6, norm1_g, w_in, q_norm_g, k_norm_g, kidx_norm_g, tm):
    B, S, D = x.shape
    wmain = w_in[:, :2560].astype(BF16)
    widx = jnp.concatenate([w_in[:, 2560:3144], jnp.zeros((D, 640 - 584), F32)], axis=1)
    widx_hi = widx.astype(BF16)
    widx_lo = (widx - widx_hi.astype(F32)).astype(BF16)
    head = jnp.arange(D_ATTN) // HEAD_DIM
    bd = jnp.where(head[:, None] == head[None, :], 1.0 / HEAD_DIM, 0.0).astype(BF16)
    qg = jnp.tile(q_norm_g, N_HEADS).reshape(1, D_ATTN)
    kg = jnp.tile(k_norm_g, N_HEADS).reshape(1, D_ATTN)
    kig = jnp.concatenate([kidx_norm_g, jnp.zeros((LANES - IDX_DIM,), F32)]).reshape(1, LANES)
    const = lambda shape: pl.BlockSpec(shape, lambda b, s: (0,) * len(shape))
    row = lambda n: pl.BlockSpec((1, tm, n), lambda b, s: (b, s, 0))
    return pl.pallas_call(
        _inproj_kernel,
        grid=(B, S // tm),
        in_specs=[row(D), pl.BlockSpec((1, 6, D), lambda b, s: (b, 0, 0)), const((1, D)),
                  const((D, 2560)), const((D, 640)), const((D, 640)), const((D_ATTN, D_ATTN)),
                  const((1, D_ATTN)), const((1, D_ATTN)), const((1, LANES))],
        out_specs=[row(512), row(512), row(512), row(512), row(512), row(512), row(LANES)],
        out_shape=[jax.ShapeDtypeStruct((B, S, 512), F32), jax.ShapeDtypeStruct((B, S, 512), F32),
                   jax.ShapeDtypeStruct((B, S, 512), BF16), jax.ShapeDtypeStruct((B, S, 512), BF16),
                   jax.ShapeDtypeStruct((B, S, 512), BF16), jax.ShapeDtypeStruct((B, S, 512), F32),
                   jax.ShapeDtypeStruct((B, S, LANES), F32)],
        compiler_params=_cparams("arbitrary", "arbitrary"),
        name="inproj",
    )(x, mod6, norm1_g.reshape(1, D), wmain, widx_hi, widx_lo, bd, qg, kg, kig)


def _rglru_kernel(xr_ref, xg_ref, cw_ref, cb_ref, wa_hi_ref, wa_lo_ref, wx_hi_ref, wx_lo_ref,
                  ba_ref, bx_ref, lam_ref, g_ref, y_ref, ext_ref, hlast_ref, *, ts):
    @pl.when(pl.program_id(1) == 0)
    def _():
        ext_ref[0:8, :] = jnp.zeros((8, D_RNN), F32)
        hlast_ref[...] = jnp.zeros((1, D_RNN), F32)

    xr = xr_ref[0]
    ext_ref[8:8 + ts, :] = xr
    cw = cw_ref[...]
    xc = cb_ref[...] + cw[0:1] * ext_ref[5:5 + ts, :]
    for j in range(1, CONV_WIDTH):
        xc = xc + cw[j:j + 1] * ext_ref[5 + j:5 + j + ts, :]
    ext_ref[0:8, :] = xr[ts - 8:ts]

    xc_hi, xc_lo = _split(xc)
    r = jax.nn.sigmoid(_dot3(xc_hi, xc_lo, wa_hi_ref[...], wa_lo_ref[...]) + ba_ref[...])
    i = jax.nn.sigmoid(_dot3(xc_hi, xc_lo, wx_hi_ref[...], wx_lo_ref[...]) + bx_ref[...])
    lam = lam_ref[...]
    sp = jnp.maximum(-lam, 0.0) + jnp.log1p(jnp.exp(-jnp.abs(lam)))
    log_a = (-RG_C) * r * sp
    a = jnp.exp(log_a)
    mult = jnp.sqrt(-jnp.tanh(log_a) * (a * a + 1.0))
    u = mult * i * xc

    row = lax.broadcasted_iota(jnp.int32, (ts, D_RNN), 0)
    acc_a, acc_h = a, u
    d = 1
    while d < ts:
        sh_a = pltpu.roll(acc_a, d, 0)
        sh_h = pltpu.roll(acc_h, d, 0)
        ok = row >= d
        acc_h = jnp.where(ok, acc_a * sh_h + acc_h, acc_h)
        acc_a = jnp.where(ok, acc_a * sh_a, acc_a)
        d *= 2
    hseq = acc_h + acc_a * hlast_ref[...]
    hlast_ref[...] = hseq[ts - 1:ts]

    xg = xg_ref[0]
    gelu = 0.5 * xg * (1.0 + jnp.tanh(0.7978845608028654 * (xg + 0.044715 * (xg * xg * xg))))
    y_ref[0] = _rms(gelu * hseq, g_ref[...]).astype(BF16)


def _block_diag(w):
    H, n, _ = w.shape
    eye = jnp.eye(H, dtype=w.dtype)
    return (eye[:, None, :, None] * w[:, :, None, :]).reshape(H * n, H * n)


def _rglru(xr, xg, conv_w, conv_b, w_a, b_a, w_x, b_x, lam, g, ts):
    B, S, _ = xr.shape
    wa_hi, wa_lo = _split(_block_diag(w_a))
    wx_hi, wx_lo = _split(_block_diag(w_x))
    const = lambda shape: pl.BlockSpec(shape, lambda b, s: (0,) * len(shape))
    row = pl.BlockSpec((1, ts, D_RNN), lambda b, s: (b, s, 0))
    vec = lambda a: a.reshape(1, D_RNN)
    return pl.pallas_call(
        functools.partial(_rglru_kernel, ts=ts),
        grid=(B, S // ts),
        in_specs=[row, row, const((CONV_WIDTH, D_RNN)), const((1, D_RNN)),
                  const((D_RNN, D_RNN)), const((D_RNN, D_RNN)), const((D_RNN, D_RNN)), const((D_RNN, D_RNN)),
                  const((1, D_RNN)), const((1, D_RNN)), const((1, D_RNN)), const((1, D_RNN))],
        out_specs=row,
        out_shape=jax.ShapeDtypeStruct((B, S, D_RNN), BF16),
        scratch_shapes=[pltpu.VMEM((ts + 8, D_RNN), F32), pltpu.VMEM((1, D_RNN), F32)],
        compiler_params=_cparams("arbitrary", "arbitrary"),
        name="rglru",
    )(xr, xg, conv_w, vec(conv_b), wa_hi, wa_lo, wx_hi, wx_lo, vec(b_a), vec(b_x), vec(lam), vec(g))


def _dsa_kernel(ki_ref, k_ref, vt_ref, qi_ref, q_ref, w_ref, y_ref,
                sc_ref, qh_ref, acc_ref, lga_ref, lgb_ref, *, qb, tk, n_sel, seq):
    j = pl.program_id(1)
    n_kt = ((j + 1) * qb + tk - 1) // tk
    q_idx = j * qb + lax.broadcasted_iota(jnp.int32, (1, qb), 1)
    q_chunk = jnp.right_shift(q_idx, CHUNK_SHIFT)

    def key_index(kt):
        return kt * tk + lax.broadcasted_iota(jnp.int32, (tk, 1), 0)

    w = w_ref[0]

    def fold8(a):
        return a.reshape(tk // SUBLANES, SUBLANES, qb)

    def count8(off, pred):
        rows = tk // REDUCE_CHAINS
        parts = []
        for i in range(REDUCE_CHAINS):
            blk = sc_ref[pl.ds(off + i * rows, rows), :]
            ind = jnp.where(pred(blk, i * rows), 1.0, 0.0)
            parts.append(jnp.sum(ind.reshape(rows // SUBLANES, SUBLANES, qb), axis=0))
        while len(parts) > 1:
            parts = [parts[i] + parts[i + 1] for i in range(0, len(parts), 2)]
        return parts[0]

    def score_tile(kt, carry):
        smin, smax = carry
        off = pl.multiple_of(kt * tk, tk)
        ki = ki_ref[0, pl.ds(off, tk), :]
        sc = jnp.zeros((tk, qb), F32)
        for h in range(IDX_HEADS):
            s = _dot(ki, qi_ref[0, h])
            sc = sc + w[h:h + 1] * jnp.maximum(s, 0.0)
        adm = jnp.right_shift(key_index(kt), CHUNK_SHIFT) <= q_chunk
        sc_ref[pl.ds(off, tk), :] = jnp.where(adm, sc, -jnp.inf)
        smin = jnp.minimum(smin, jnp.min(fold8(jnp.where(adm, sc, jnp.inf)), axis=0))
        smax = jnp.maximum(smax, jnp.max(fold8(jnp.where(adm, sc, -jnp.inf)), axis=0))
        return smin, smax

    smin8, smax8 = lax.fori_loop(0, n_kt, score_tile,
                                 (jnp.full((SUBLANES, qb), jnp.inf, F32),
                                  jnp.full((SUBLANES, qb), -jnp.inf, F32)))
    smin = jnp.min(smin8, axis=0, keepdims=True)
    smax = jnp.max(smax8, axis=0, keepdims=True)

    def count_ge(t):
        def body(kt, c8):
            off = pl.multiple_of(kt * tk, tk)
            return c8 + count8(off, lambda blk, r0: blk >= t)
        c8 = lax.fori_loop(0, n_kt, body, jnp.zeros((SUBLANES, qb), F32))
        return jnp.sum(c8, axis=0, keepdims=True)

    n_adm = (q_chunk + 1) * CHUNK
    take_all = n_adm <= n_sel
    nf = float(n_sel)

    def probe_step(st, t):
        lo, hi, cnt_hi, thr, exact, done_f = st
        active = done_f == 0.0
        c = count_ge(t)
        inside = jnp.logical_and(t > lo, t < hi)
        stuck = jnp.logical_and(active, jnp.logical_not(inside))
        probed = jnp.logical_and(active, inside)
        hit = jnp.logical_and(probed, c == nf)
        move_lo = jnp.logical_and(probed, c > nf)
        move_hi = jnp.logical_and(probed, c < nf)
        finished = jnp.logical_or(hit, stuck)
        return (jnp.where(move_lo, t, lo), jnp.where(move_hi, t, hi), jnp.where(move_hi, c, cnt_hi),
                jnp.where(hit, t, jnp.where(stuck, lo, thr)), jnp.where(hit, 1.0, exact),
                jnp.where(finished, 1.0, done_f))

    def midpoint(lo, hi):
        return lo + (hi - lo) * 0.5

    zeros = jnp.zeros((1, qb), F32)
    take_all_f = jnp.where(take_all, 1.0, 0.0)
    c_top = count_ge(smax)
    top_hit = c_top == nf
    top_ge = c_top >= nf
    live = jnp.logical_not(take_all)
    st = (jnp.where(jnp.logical_and(live, top_ge), smax, smin), smax,
          jnp.where(jnp.logical_and(live, jnp.logical_not(top_ge)), c_top, zeros),
          jnp.where(jnp.logical_and(live, top_hit), smax, jnp.full((1, qb), -jnp.inf, F32)),
          jnp.where(jnp.logical_and(live, top_hit), 1.0, take_all_f),
          jnp.where(jnp.logical_and(live, top_hit), 1.0, take_all_f))
    for special in (0.0, TINY_F32):
        lo, hi = st[0], st[1]
        sp = jnp.full((1, qb), special, F32)
        st = probe_step(st, jnp.where(jnp.logical_and(sp > lo, sp < hi), sp, midpoint(lo, hi)))

    def bis_cond(st):
        return jnp.min(st[5]) < 1.0

    def bis_body(st):
        return probe_step(st, midpoint(st[0], st[1]))

    lo, hi, cnt_hi, thr, exact, _ = lax.while_loop(bis_cond, bis_body, st)
    need = nf - cnt_hi

    def count_tie(jcut):
        def body(kt, c8):
            off = pl.multiple_of(kt * tk, tk)

            def tied_before_cut(blk, r0):
                key = off + r0 + lax.broadcasted_iota(jnp.int32, (blk.shape[0], 1), 0)
                return jnp.logical_and(blk == thr, key <= jcut)
            return c8 + count8(off, tied_before_cut)
        c8 = lax.fori_loop(0, n_kt, body, jnp.zeros((SUBLANES, qb), F32))
        return jnp.sum(c8, axis=0, keepdims=True)

    def tie_cond(st):
        jlo, jhi = st
        return jnp.max(jnp.where(jnp.logical_and(exact == 0.0, jlo < jhi), 1.0, 0.0)) > 0.0

    def tie_body(st):
        jlo, jhi = st
        jmid = jnp.right_shift(jlo + jhi, 1)
        ok = count_tie(jmid) >= need
        return jnp.where(ok, jlo, jmid + 1), jnp.where(ok, jmid, jhi)

    jlo, _ = lax.while_loop(tie_cond, tie_body,
                            (jnp.zeros((1, qb), jnp.int32), jnp.full((1, qb), seq - 1, jnp.int32)))
    jcut = jnp.where(take_all, -1, jnp.where(exact > 0.0, seq, jlo))

    acc_ref[...] = jnp.zeros((N_HEADS * V_ROWS, qb), F32)
    zero_half = jnp.zeros((HEAD_DIM, qb), BF16)
    for p in range(N_HEADS // 2):
        qh_ref[2 * p] = jnp.concatenate([q_ref[0, p * LANES:p * LANES + HEAD_DIM, :], zero_half], axis=0)
        qh_ref[2 * p + 1] = jnp.concatenate([zero_half, q_ref[0, p * LANES + HEAD_DIM:(p + 1) * LANES, :]],
                                            axis=0)

    def masked_logits(kt, dst_ref):
        kt = jnp.minimum(kt, n_kt - 1)
        off = pl.multiple_of(kt * tk, tk)
        blk = sc_ref[pl.ds(off, tk), :]
        sel = jnp.logical_or(blk > thr, jnp.logical_and(blk == thr, key_index(kt) <= jcut))
        bias = jnp.where(sel, 0.0, NEG_BIG).astype(BF16)
        for h in range(N_HEADS):
            p = h // 2
            kp = k_ref[0, pl.ds(off, tk), p * LANES:(p + 1) * LANES]
            dst_ref[h] = _dot(kp, qh_ref[h]).astype(BF16) + bias

    def softmax_pv(kt, src_ref, ms):
        off = pl.multiple_of(kt * tk, tk)
        m_new = []
        for h in range(N_HEADS):
            logits = src_ref[h]
            m_h = jnp.maximum(ms[h], jnp.max(logits, axis=0, keepdims=True).astype(F32))
            alpha = jnp.exp2(ms[h] - m_h)
            probs = jnp.exp2(logits - m_h.astype(BF16))
            vt = vt_ref[0, h * V_ROWS:(h + 1) * V_ROWS, pl.ds(off, tk)]
            rows = slice(h * V_ROWS, (h + 1) * V_ROWS)
            acc_ref[rows, :] = alpha * acc_ref[rows, :] + _dot(vt, probs)
            m_new.append(m_h)
        return tuple(m_new)

    masked_logits(0, lga_ref)

    def tile_pair(i, ms):
        kt0 = 2 * i
        masked_logits(kt0 + 1, lgb_ref)
        ms = softmax_pv(kt0, lga_ref, ms)

        def second(ms):
            masked_logits(kt0 + 2, lga_ref)
            return softmax_pv(kt0 + 1, lgb_ref, ms)

        return lax.cond(kt0 + 1 < n_kt, second, lambda ms: ms, ms)

    m0 = tuple(jnp.full((1, qb), NEG_BIG, F32) for _ in range(N_HEADS))
    lax.fori_loop(0, (n_kt + 1) // 2, tile_pair, m0)
    for h in range(N_HEADS):
        num = acc_ref[h * V_ROWS:h * V_ROWS + HEAD_DIM, :]
        den = acc_ref[h * V_ROWS + HEAD_DIM:h * V_ROWS + HEAD_DIM + 1, :]
        y_ref[0, h * HEAD_DIM:(h + 1) * HEAD_DIM, :] = num / den


def _dsa(ki_ext, k, vt, qi_ext, q, w_t, qb, tk):
    B, S, _ = k.shape
    n_sel = min(TOPK_KEYS_MAX, S // 4)
    resident = lambda shape: pl.BlockSpec(shape, lambda b, j: (b, 0, 0), pipeline_mode=pl.Buffered(1))
    return pl.pallas_call(
        functools.partial(_dsa_kernel, qb=qb, tk=tk, n_sel=n_sel, seq=S),
        grid=(B, S // qb),
        in_specs=[resident((1, S, IDX_K)), resident((1, S, D_ATTN)), resident((1, N_HEADS * V_ROWS, S)),
                  pl.BlockSpec((1, IDX_HEADS, IDX_K, qb), lambda b, j: (b, 0, 0, j)),
                  pl.BlockSpec((1, D_ATTN, qb), lambda b, j: (b, 0, j)),
                  pl.BlockSpec((1, IDX_HEADS, qb), lambda b, j: (b, 0, j))],
        out_specs=pl.BlockSpec((1, D_ATTN, qb), lambda b, j: (b, 0, j)),
        out_shape=jax.ShapeDtypeStruct((B, D_ATTN, S), F32),
        scratch_shapes=[pltpu.VMEM((S, qb), F32), pltpu.VMEM((N_HEADS, LANES, qb), BF16),
                        pltpu.VMEM((N_HEADS * V_ROWS, qb), F32),
                        pltpu.VMEM((N_HEADS, tk, qb), BF16), pltpu.VMEM((N_HEADS, tk, qb), BF16)],
        compiler_params=_cparams("arbitrary", "arbitrary"),
        name="dsa",
    )(ki_ext, k, vt, qi_ext, q, w_t)


def _outproj_kernel(x_ref, yr_ref, ya_ref, mod_ref, ag_ref, wor_ref, woa_ref, n2g_ref,
                    wr_hi_ref, wr_lo_ref, br_ref,
                    x1_ref, h2_ref, meta_ref, cnt_ref, base_ref, *, tm):
    @pl.when(jnp.logical_and(pl.program_id(0) == 0, pl.program_id(1) == 0))
    def _():
        base_ref[...] = jnp.zeros((1, LANES), F32)

    mod = mod_ref[0]
    ya = _rms(ya_ref[0], ag_ref[...]).astype(BF16)
    mix = _dot(yr_ref[0], wor_ref[...]) + _dot(ya, woa_ref[...])
    x1 = x_ref[0] + mod[2:3] * mix
    x1_ref[0] = x1
    h2 = _rms(x1, n2g_ref[...]) * (1.0 + mod[4:5]) + mod[3:4]
    h2_ref[0] = _pack_bf16_pairs(h2)
    h2_hi, h2_lo = _split(h2)
    logits = _dot3(h2_hi, h2_lo, wr_hi_ref[...], wr_lo_ref[...]) + br_ref[...]

    lane = lax.broadcasted_iota(jnp.int32, (tm, LANES), 1).astype(F32)
    work = logits
    vals, onehots, idxs = [], [], []
    for _ in range(TOPK_EXPERTS):
        mx = jnp.max(work, axis=-1, keepdims=True)
        idx = jnp.min(jnp.where(work == mx, lane, float(LANES)), axis=-1, keepdims=True)
        oh = lane == idx
        vals.append(mx)
        idxs.append(idx)
        onehots.append(oh)
        work = jnp.where(oh, -jnp.inf, work)
    es = [jnp.exp(v - vals[0]) for v in vals]
    denom = es[0] + es[1] + es[2] + es[3]
    assigned = jnp.zeros((tm, LANES), F32)
    for oh in onehots:
        assigned = assigned + jnp.where(oh, 1.0, 0.0)
    r_i = lax.broadcasted_iota(jnp.int32, (tm, tm), 0)
    c_i = lax.broadcasted_iota(jnp.int32, (tm, tm), 1)
    ltri = jnp.where(c_i < r_i, 1.0, 0.0).astype(BF16)
    prior = _dot(ltri, assigned.astype(BF16)) + base_ref[...]
    meta = jnp.zeros((tm, LANES), F32)
    for jj in range(TOPK_EXPERTS):
        rank = jnp.sum(jnp.where(onehots[jj], prior, 0.0), axis=-1, keepdims=True)
        meta = jnp.where(lane == jj, idxs[jj], meta)
        meta = jnp.where(lane == TOPK_EXPERTS + jj, es[jj] / denom, meta)
        meta = jnp.where(lane == 2 * TOPK_EXPERTS + jj, rank, meta)
    meta_ref[0] = meta
    base = base_ref[...] + jnp.sum(assigned, axis=0, keepdims=True)
    base_ref[...] = base
    cnt_ref[...] = base


def _outproj(x, y_rnn, y_attn, mod6, attn_out_g, w_out, norm2_g, w_router, b_router, tm):
    B, S, D = x.shape
    wor = w_out[:D_RNN].astype(BF16)
    woa = w_out[D_RNN:].astype(BF16)
    wr = jnp.zeros((D, LANES), F32).at[:, :N_EXPERTS].set(w_router)
    wr_hi, wr_lo = _split(wr)
    br = jnp.full((1, LANES), NEG_BIG, F32).at[0, :N_EXPERTS].set(b_router)
    const = lambda shape: pl.BlockSpec(shape, lambda b, s: (0,) * len(shape))
    row = lambda n: pl.BlockSpec((1, tm, n), lambda b, s: (b, s, 0))
    return pl.pallas_call(
        functools.partial(_outproj_kernel, tm=tm),
        grid=(B, S // tm),
        in_specs=[row(D), row(D_RNN), row(D_ATTN), pl.BlockSpec((1, 6, D), lambda b, s: (b, 0, 0)),
                  const((1, D_ATTN)), const((D_RNN, D)), const((D_ATTN, D)), const((1, D)),
                  const((D, LANES)), const((D, LANES)), const((1, LANES))],
        out_specs=[row(D), row(D // 2), row(LANES), const((1, LANES))],
        out_shape=[jax.ShapeDtypeStruct((B, S, D), F32), jax.ShapeDtypeStruct((B, S, D // 2), jnp.int32),
                   jax.ShapeDtypeStruct((B, S, LANES), F32), jax.ShapeDtypeStruct((1, LANES), F32)],
        scratch_shapes=[pltpu.VMEM((1, LANES), F32)],
        compiler_params=_cparams("arbitrary", "arbitrary"),
        name="outproj",
    )(x, y_rnn, y_attn, mod6, attn_out_g.reshape(1, D_ATTN), wor, woa, norm2_g.reshape(1, D),
      wr_hi, wr_lo, br)


GLU_GROUP = 2 * LANES


def _degroup_kernel(w_ref, p_ref, o_ref):
    p = p_ref[...]
    for g in range(2 * D_FF // GLU_GROUP):
        cols = slice(g * GLU_GROUP, (g + 1) * GLU_GROUP)
        o_ref[0, :, cols] = _dot(w_ref[0, :, cols].astype(BF16), p).astype(BF16)


def _degroup_w1(w1):
    E, D, F2 = w1.shape
    src = jnp.arange(GLU_GROUP)
    dst = jnp.where(src % 2 == 0, src // 2, LANES + src // 2)
    perm = (dst[:, None] == jnp.arange(GLU_GROUP)[None, :]).astype(BF16)
    tr = 512
    return pl.pallas_call(
        _degroup_kernel,
        grid=(E, D // tr),
        in_specs=[pl.BlockSpec((1, tr, F2), lambda e, r: (e, r, 0)),
                  pl.BlockSpec((GLU_GROUP, GLU_GROUP), lambda e, r: (0, 0))],
        out_specs=pl.BlockSpec((1, tr, F2), lambda e, r: (e, r, 0)),
        out_shape=jax.ShapeDtypeStruct((E, D, F2), BF16),
        compiler_params=_cparams("arbitrary", "arbitrary"),
        name="degroup_w1",
    )(w1, perm)


def _expert_kernel(be_ref, nb_ref, x_ref, w1_ref, b1_ref, w2_ref, b2_ref, y_ref):
    del be_ref

    @pl.when(pl.program_id(0) < nb_ref[0])
    def _():
        x = _unpack_bf16_pairs(x_ref[...]).astype(BF16)
        u = _dot(x, w1_ref[0]) + b1_ref[0]
        acts = []
        for g in range(2 * D_FF // GLU_GROUP):
            ug = jnp.minimum(u[:, g * GLU_GROUP:g * GLU_GROUP + LANES], SWIGLU_LIMIT)
            ul = jnp.clip(u[:, g * GLU_GROUP + LANES:(g + 1) * GLU_GROUP], -SWIGLU_LIMIT, SWIGLU_LIMIT)
            acts.append((ug * jax.nn.sigmoid(SWIGLU_ALPHA * ug) * (ul + 1.0)).astype(BF16))
        act = jnp.concatenate(acts, axis=1)
        y_ref[...] = _pack_bf16_pairs(_dot(act, w2_ref[0]) + b2_ref[0])

    @pl.when(pl.program_id(0) >= nb_ref[0])
    def _():
        y_ref[...] = jnp.zeros(y_ref.shape, y_ref.dtype)


def _experts(x_sorted, block_e, n_used, w1, b1, w2, b2, mb):
    n_rows, half = x_sorted.shape
    D = 2 * half
    E = w1.shape[0]
    w1p = _degroup_w1(w1)
    ng = 2 * D_FF // GLU_GROUP
    b1p = jnp.swapaxes(b1.reshape(E, ng, LANES, 2), 2, 3).reshape(E, 1, 2 * D_FF)
    wspec = lambda k, n: pl.BlockSpec((1, k, n), lambda i, be, nb: (be[i], 0, 0))
    grid_spec = pltpu.PrefetchScalarGridSpec(
        num_scalar_prefetch=2,
        grid=(n_rows // mb,),
        in_specs=[pl.BlockSpec((mb, half), lambda i, be, nb: (jnp.minimum(i, nb[0] - 1), 0)),
                  wspec(D, 2 * D_FF), wspec(1, 2 * D_FF), wspec(D_FF, D), wspec(1, D)],
        out_specs=pl.BlockSpec((mb, half), lambda i, be, nb: (i, 0)),
    )
    return pl.pallas_call(
        _expert_kernel,
        grid_spec=grid_spec,
        out_shape=jax.ShapeDtypeStruct((n_rows, half), jnp.int32),
        compiler_params=_cparams("arbitrary"),
        name="experts",
    )(block_e, n_used, x_sorted, w1p, b1p, w2.astype(BF16), b2.reshape(E, 1, D))


def _combine_kernel(x1_ref, yg_ref, meta_ref, mod_ref, o_ref):
    meta = meta_ref[0]
    ff = meta[:, TOPK_EXPERTS:TOPK_EXPERTS + 1] * _unpack_bf16_pairs(yg_ref[0])
    for jj in range(1, TOPK_EXPERTS):
        ff = ff + meta[:, TOPK_EXPERTS + jj:TOPK_EXPERTS + jj + 1] * _unpack_bf16_pairs(yg_ref[jj])
    o_ref[0] = x1_ref[0] + mod_ref[0][5:6] * ff


def _combine(x1, yg, meta, mod6, tm):
    B, S, D = x1.shape
    nt = S // tm
    return pl.pallas_call(
        _combine_kernel,
        grid=(B, nt),
        in_specs=[pl.BlockSpec((1, tm, D), lambda b, s: (b, s, 0)),
                  pl.BlockSpec((TOPK_EXPERTS, tm, D // 2), lambda b, s: (0, b * nt + s, 0)),
                  pl.BlockSpec((1, tm, LANES), lambda b, s: (b, s, 0)),
                  pl.BlockSpec((1, 6, D), lambda b, s: (b, 0, 0))],
        out_specs=pl.BlockSpec((1, tm, D), lambda b, s: (b, s, 0)),
        out_shape=jax.ShapeDtypeStruct((B, S, D), F32),
        compiler_params=_cparams("arbitrary", "arbitrary"),
        name="combine",
    )(x1, yg, meta, mod6)


def _tile(n, pref):
    t = pref
    while n % t:
        t //= 2
    return t


def _layer(x, c, w_ada, b_ada, norm1_g, w_in, conv_w, conv_b, w_rg_a, b_rg_a, w_rg_x, b_rg_x,
           lru_lambda, q_norm_g, k_norm_g, kidx_norm_g, rg_out_g, attn_out_g, w_out, norm2_g,
           w_router, b_router, w1, b1, w2, b2):
    B, S, D = x.shape
    T = B * S
    tm = _tile(S, 512)
    mod6 = _ada(c, w_ada, b_ada).reshape(B, 6, D)

    xr, xg, q, k, v, qi, kiw = _inproj(x, mod6, norm1_g, w_in, q_norm_g, k_norm_g, kidx_norm_g, tm)
    y_rnn = _rglru(xr, xg, conv_w, conv_b, w_rg_a, b_rg_a, w_rg_x, b_rg_x, lru_lambda, rg_out_g,
                   _tile(S, 256))

    ki = kiw[..., :IDX_DIM]
    ki_hi, ki_lo = _split(ki)
    ki_ext = jnp.concatenate([ki_hi, ki_hi, ki_lo, jnp.zeros_like(ki_hi)], axis=-1)
    qi_t = jnp.transpose(qi.reshape(B, S, IDX_HEADS, IDX_DIM), (0, 2, 3, 1))
    qi_hi, qi_lo = _split(qi_t)
    qi_ext = jnp.concatenate([qi_hi, qi_lo, qi_hi, jnp.zeros_like(qi_hi)], axis=2)
    q_t = jnp.swapaxes(q, 1, 2)
    w_t = jnp.swapaxes(kiw[..., IDX_DIM:IDX_DIM + IDX_HEADS], 1, 2)
    v_h = v.reshape(B, S, N_HEADS, HEAD_DIM)
    v_ext = jnp.concatenate([v_h, jnp.ones((B, S, N_HEADS, 1), BF16),
                             jnp.zeros((B, S, N_HEADS, V_ROWS - HEAD_DIM - 1), BF16)], axis=-1)
    vt = jnp.swapaxes(v_ext.reshape(B, S, N_HEADS * V_ROWS), 1, 2)
    qb = _tile(S, 256)
    y_attn_t = _dsa(ki_ext, k, vt, qi_ext, q_t, w_t, qb, _tile(S, 512))
    y_attn = jnp.swapaxes(y_attn_t, 1, 2)

    x1, h2, meta, cnt = _outproj(x, y_rnn, y_attn, mod6, attn_out_g, w_out, norm2_g,
                                 w_router, b_router, tm)

    mb = 512
    top_e = meta[..., 0:TOPK_EXPERTS].astype(jnp.int32).reshape(T, TOPK_EXPERTS)
    rank = meta[..., 2 * TOPK_EXPERTS:3 * TOPK_EXPERTS].astype(jnp.int32).reshape(T, TOPK_EXPERTS)
    counts = cnt[0, :N_EXPERTS].astype(jnp.int32)
    padded = (counts + mb - 1) // mb * mb
    pad_end = jnp.cumsum(padded)
    pad_start = pad_end - padded
    dest = pad_start[top_e] + rank
    n_rows = T * TOPK_EXPERTS + N_EXPERTS * mb
    n_blocks = n_rows // mb
    block_start = jnp.arange(n_blocks, dtype=jnp.int32) * mb
    block_e = jnp.minimum(jnp.sum((pad_end[None, :] <= block_start[:, None]).astype(jnp.int32), axis=1),
                          N_EXPERTS - 1)
    tok = jnp.broadcast_to(jnp.arange(T, dtype=jnp.int32)[:, None], (T, TOPK_EXPERTS))
    row_tok = jnp.zeros((n_rows,), jnp.int32).at[dest.reshape(-1)].set(
        tok.reshape(-1), unique_indices=True, mode="promise_in_bounds")
    x_sorted = h2.reshape(T, D // 2)[row_tok]
    n_used = (pad_end[N_EXPERTS - 1:] // mb).astype(jnp.int32)
    y_sorted = _experts(x_sorted, block_e, n_used, w1, b1, w2, b2, mb)
    yg = y_sorted[dest.T]
    return _combine(x1, yg, meta, mod6, tm)


def kernel(x, c, w_ada, b_ada, norm1_g, w_in, conv_w, conv_b, w_rg_a, b_rg_a, w_rg_x, b_rg_x,
           lru_lambda, q_norm_g, k_norm_g, kidx_norm_g, rg_out_g, attn_out_g, w_out, norm2_g,
           w_router, b_router, w1, b1, w2, b2):
    depth = w_ada.shape[0]
    for l in range(depth):
        x = _layer(x, c, w_ada[l], b_ada[l], norm1_g[l], w_in[l], conv_w[l], conv_b[l],
                   w_rg_a[l], b_rg_a[l], w_rg_x[l], b_rg_x[l], lru_lambda[l], q_norm_g[l],
                   k_norm_g[l], kidx_norm_g[l], rg_out_g[l], attn_out_g[l], w_out[l], norm2_g[l],
                   w_router[l], b_router[l], w1[l], b1[l], w2[l], b2[l])
    return x
```

```python
import functools

import jax
import jax.numpy as jnp
from jax import lax
from jax.experimental import pallas as pl
from jax.experimental.pallas import tpu as pltpu

D_MODEL = 1024
CHUNK = 64
CHUNK_SHIFT = 6
D_RNN = 512
RNN_BLOCKS = 8
RNN_BLOCK = 64
CONV_WIDTH = 4
RG_C = 8.0
N_HEADS = 8
HEAD_DIM = 64
D_ATTN = 512
IDX_HEADS = 8
IDX_DIM = 64
TOPK_KEYS_MAX = 256
N_EXPERTS = 32
TOPK_EXPERTS = 4
D_FF = 1024
SWIGLU_ALPHA = 1.702
SWIGLU_LIMIT = 7.0
EPS = 1e-6

LANES = 128
SUBLANES = 8
TINY_F32 = 1.1754943508222875e-38
REDUCE_CHAINS = 8
IDX_K = 256
NEG_BIG = -(2.0 ** 100)
LOG2E = 1.4426950408889634
V_ROWS = 80
VMEM_LIMIT = 56 * 1024 * 1024

F32 = jnp.float32
BF16 = jnp.bfloat16


def _dot(a, b):
    return jnp.dot(a, b, preferred_element_type=F32)


def _dot_nt(a, b):
    return lax.dot_general(a, b, (((1,), (1,)), ((), ())), preferred_element_type=F32)


def _split(a):
    hi = a.astype(BF16)
    lo = (a - hi.astype(F32)).astype(BF16)
    return hi, lo


def _dot3(a_hi, a_lo, b_hi, b_lo):
    return _dot(a_hi, b_hi) + (_dot(a_hi, b_lo) + _dot(a_lo, b_hi))


def _pack_bf16_pairs(a):
    n = a.shape[1] // 2
    hi = lax.bitcast_convert_type(a[:, :n].astype(BF16).astype(F32), jnp.int32)
    lo = lax.bitcast_convert_type(a[:, n:].astype(BF16).astype(F32), jnp.int32)
    return jnp.bitwise_or(hi, lax.shift_right_logical(lo, 16))


def _unpack_bf16_pairs(u):
    hi = lax.bitcast_convert_type(jnp.bitwise_and(u, -65536), F32)
    lo = lax.bitcast_convert_type(lax.shift_left(u, 16), F32)
    return jnp.concatenate([hi, lo], axis=1)


def _rms(x, g):
    ms = jnp.mean(x * x, axis=-1, keepdims=True)
    return x * lax.rsqrt(ms + EPS) * g


def _cparams(*sem):
    return pltpu.CompilerParams(dimension_semantics=sem, vmem_limit_bytes=VMEM_LIMIT)


def _ada_kernel(c_ref, w_ref, b_ref, o_ref):
    c = c_ref[...]
    s = c * jax.nn.sigmoid(c)
    s_hi, s_lo = _split(s)
    w_hi, w_lo = _split(w_ref[...])
    o_ref[...] = _dot3(s_hi, s_lo, w_hi, w_lo) + b_ref[...]


def _ada(c, w_ada, b_ada):
    B, D = c.shape
    N = w_ada.shape[1]
    rows = 16
    c_pad = jnp.zeros((rows, D), F32).at[:B].set(c)
    tn = 1024
    out = pl.pallas_call(
        _ada_kernel,
        grid=(N // tn,),
        in_specs=[pl.BlockSpec((rows, D), lambda j: (0, 0)),
                  pl.BlockSpec((D, tn), lambda j: (0, j)),
                  pl.BlockSpec((1, tn), lambda j: (0, j))],
        out_specs=pl.BlockSpec((rows, tn), lambda j: (0, j)),
        out_shape=jax.ShapeDtypeStruct((rows, N), F32),
        compiler_params=_cparams("arbitrary"),
        name="ada",
    )(c_pad, w_ada, b_ada.reshape(1, N))
    return out[:B]


def _inproj_kernel(x_ref, mod_ref, g1_ref, wmain_ref, widx_hi_ref, widx_lo_ref, bd_ref,
                   qg_ref, kg_ref, kig_ref,
                   xr_ref, xg_ref, q_ref, k_ref, v_ref, qi_ref, kiw_ref):
    x = x_ref[0]
    mod = mod_ref[0]
    h = _rms(x, g1_ref[...]) * (1.0 + mod[1:2]) + mod[0:1]
    h_hi, h_lo = _split(h)
    bd = bd_ref[...]

    def head_norm(z, g):
        sq_hi, sq_lo = _split(z * z)
        ms = _dot(sq_hi, bd) + _dot(sq_lo, bd)
        return z * lax.rsqrt(ms + EPS) * g

    xr_ref[0] = _dot(h_hi, wmain_ref[:, 0:512])
    xg_ref[0] = _dot(h_hi, wmain_ref[:, 512:1024])
    zq = _dot(h_hi, wmain_ref[:, 1024:1536])
    q_ref[0] = (head_norm(zq, qg_ref[...]) * (HEAD_DIM ** -0.5 * LOG2E)).astype(BF16)
    zk = _dot(h_hi, wmain_ref[:, 1536:2048])
    k_ref[0] = head_norm(zk, kg_ref[...]).astype(BF16)
    v_ref[0] = _dot(h_hi, wmain_ref[:, 2048:2560]).astype(BF16)

    zi = _dot3(h_hi, h_lo, widx_hi_ref[...], widx_lo_ref[...])
    qi_ref[0] = zi[:, 0:512] * (IDX_DIM ** -0.5)
    kw = zi[:, 512:640]
    lane = lax.broadcasted_iota(jnp.int32, kw.shape, 1)
    is_ki = lane < IDX_DIM
    ms = jnp.sum(jnp.where(is_ki, kw * kw, 0.0), axis=-1, keepdims=True) * (1.0 / IDX_DIM)
    kiw_ref[0] = jnp.where(is_ki, kw * lax.rsqrt(ms + EPS) * kig_ref[...], kw * (IDX_HEADS ** -0.5))


def _inproj(x, mod6, norm1_g, w_in, q_norm_g, k_norm_g, kidx_norm_g, tm):
    B, S, D = x.shape
    wmain = w_in[:, :2560].astype(BF16)
    widx = jnp.concatenate([w_in[:, 2560:3144], jnp.zeros((D, 640 - 584), F32)], axis=1)
    widx_hi = widx.astype(BF16)
    widx_lo = (widx - widx_hi.astype(F32)).astype(BF16)
    head = jnp.arange(D_ATTN) // HEAD_DIM
    bd = jnp.where(head[:, None] == head[None, :], 1.0 / HEAD_DIM, 0.0).astype(BF16)
    qg = jnp.tile(q_norm_g, N_HEADS).reshape(1, D_ATTN)
    kg = jnp.tile(k_norm_g, N_HEADS).reshape(1, D_ATTN)
    kig = jnp.concatenate([kidx_norm_g, jnp.zeros((LANES - IDX_DIM,), F32)]).reshape(1, LANES)
    const = lambda shape: pl.BlockSpec(shape, lambda b, s: (0,) * len(shape))
    row = lambda n: pl.BlockSpec((1, tm, n), lambda b, s: (b, s, 0))
    return pl.pallas_call(
        _inproj_kernel,
        grid=(B, S // tm),
        in_specs=[row(D), pl.BlockSpec((1, 6, D), lambda b, s: (b, 0, 0)), const((1, D)),
                  const((D, 2560)), const((D, 640)), const((D, 640)), const((D_ATTN, D_ATTN)),
                  const((1, D_ATTN)), const((1, D_ATTN)), const((1, LANES))],
        out_specs=[row(512), row(512), row(512), row(512), row(512), row(512), row(LANES)],
        out_shape=[jax.ShapeDtypeStruct((B, S, 512), F32), jax.ShapeDtypeStruct((B, S, 512), F32),
                   jax.ShapeDtypeStruct((B, S, 512), BF16), jax.ShapeDtypeStruct((B, S, 512), BF16),
                   jax.ShapeDtypeStruct((B, S, 512), BF16), jax.ShapeDtypeStruct((B, S, 512), F32),
                   jax.ShapeDtypeStruct((B, S, LANES), F32)],
        compiler_params=_cparams("arbitrary", "arbitrary"),
        name="inproj",
    )(x, mod6, norm1_g.reshape(1, D), wmain, widx_hi, widx_lo, bd, qg, kg, kig)


def _rglru_kernel(xr_ref, xg_ref, cw_ref, cb_ref, wa_hi_ref, wa_lo_ref, wx_hi_ref, wx_lo_ref,
                  ba_ref, bx_ref, lam_ref, g_ref, y_ref, ext_ref, hlast_ref, *, ts):
    @pl.when(pl.program_id(1) == 0)
    def _():
        ext_ref[0:8, :] = jnp.zeros((8, D_RNN), F32)
        hlast_ref[...] = jnp.zeros((1, D_RNN), F32)

    xr = xr_ref[0]
    ext_ref[8:8 + ts, :] = xr
    cw = cw_ref[...]
    xc = cb_ref[...] + cw[0:1] * ext_ref[5:5 + ts, :]
    for j in range(1, CONV_WIDTH):
        xc = xc + cw[j:j + 1] * ext_ref[5 + j:5 + j + ts, :]
    ext_ref[0:8, :] = xr[ts - 8:ts]

    xc_hi, xc_lo = _split(xc)
    r = jax.nn.sigmoid(_dot3(xc_hi, xc_lo, wa_hi_ref[...], wa_lo_ref[...]) + ba_ref[...])
    i = jax.nn.sigmoid(_dot3(xc_hi, xc_lo, wx_hi_ref[...], wx_lo_ref[...]) + bx_ref[...])
    lam = lam_ref[...]
    sp = jnp.maximum(-lam, 0.0) + jnp.log1p(jnp.exp(-jnp.abs(lam)))
    log_a = (-RG_C) * r * sp
    a = jnp.exp(log_a)
    mult = jnp.sqrt(-jnp.tanh(log_a) * (a * a + 1.0))
    u = mult * i * xc

    row = lax.broadcasted_iota(jnp.int32, (ts, D_RNN), 0)
    acc_a, acc_h = a, u
    d = 1
    while d < ts:
        sh_a = pltpu.roll(acc_a, d, 0)
        sh_h = pltpu.roll(acc_h, d, 0)
        ok = row >= d
        acc_h = jnp.where(ok, acc_a * sh_h + acc_h, acc_h)
        acc_a = jnp.where(ok, acc_a * sh_a, acc_a)
        d *= 2
    hseq = acc_h + acc_a * hlast_ref[...]
    hlast_ref[...] = hseq[ts - 1:ts]

    xg = xg_ref[0]
    gelu = 0.5 * xg * (1.0 + jnp.tanh(0.7978845608028654 * (xg + 0.044715 * (xg * xg * xg))))
    y_ref[0] = _rms(gelu * hseq, g_ref[...]).astype(BF16)


def _block_diag(w):
    H, n, _ = w.shape
    eye = jnp.eye(H, dtype=w.dtype)
    return (eye[:, None, :, None] * w[:, :, None, :]).reshape(H * n, H * n)


def _rglru(xr, xg, conv_w, conv_b, w_a, b_a, w_x, b_x, lam, g, ts):
    B, S, _ = xr.shape
    wa_hi, wa_lo = _split(_block_diag(w_a))
    wx_hi, wx_lo = _split(_block_diag(w_x))
    const = lambda shape: pl.BlockSpec(shape, lambda b, s: (0,) * len(shape))
    row = pl.BlockSpec((1, ts, D_RNN), lambda b, s: (b, s, 0))
    vec = lambda a: a.reshape(1, D_RNN)
    return pl.pallas_call(
        functools.partial(_rglru_kernel, ts=ts),
        grid=(B, S // ts),
        in_specs=[row, row, const((CONV_WIDTH, D_RNN)), const((1, D_RNN)),
                  const((D_RNN, D_RNN)), const((D_RNN, D_RNN)), const((D_RNN, D_RNN)), const((D_RNN, D_RNN)),
                  const((1, D_RNN)), const((1, D_RNN)), const((1, D_RNN)), const((1, D_RNN))],
        out_specs=row,
        out_shape=jax.ShapeDtypeStruct((B, S, D_RNN), BF16),
        scratch_shapes=[pltpu.VMEM((ts + 8, D_RNN), F32), pltpu.VMEM((1, D_RNN), F32)],
        compiler_params=_cparams("arbitrary", "arbitrary"),
        name="rglru",
    )(xr, xg, conv_w, vec(conv_b), wa_hi, wa_lo, wx_hi, wx_lo, vec(b_a), vec(b_x), vec(lam), vec(g))


def _dsa_kernel(ki_ref, k_ref, vt_ref, qi_ref, q_ref, w_ref, y_ref,
                sc_ref, qh_ref, acc_ref, lga_ref, lgb_ref, *, qb, tk, n_sel, seq):
    j = pl.program_id(1)
    n_kt = ((j + 1) * qb + tk - 1) // tk
    q_idx = j * qb + lax.broadcasted_iota(jnp.int32, (1, qb), 1)
    q_chunk = jnp.right_shift(q_idx, CHUNK_SHIFT)

    def key_index(kt):
        return kt * tk + lax.broadcasted_iota(jnp.int32, (tk, 1), 0)

    w = w_ref[0]

    def fold8(a):
        return a.reshape(tk // SUBLANES, SUBLANES, qb)

    def count8(off, pred):
        rows = tk // REDUCE_CHAINS
        parts = []
        for i in range(REDUCE_CHAINS):
            blk = sc_ref[pl.ds(off + i * rows, rows), :]
            ind = jnp.where(pred(blk, i * rows), 1.0, 0.0)
            parts.append(jnp.sum(ind.reshape(rows // SUBLANES, SUBLANES, qb), axis=0))
        while len(parts) > 1:
            parts = [parts[i] + parts[i + 1] for i in range(0, len(parts), 2)]
        return parts[0]

    def score_tile(kt, carry):
        smin, smax = carry
        off = pl.multiple_of(kt * tk, tk)
        ki = ki_ref[0, pl.ds(off, tk), :]
        sc = jnp.zeros((tk, qb), F32)
        for h in range(IDX_HEADS):
            s = _dot(ki, qi_ref[0, h])
            sc = sc + w[h:h + 1] * jnp.maximum(s, 0.0)
        adm = jnp.right_shift(key_index(kt), CHUNK_SHIFT) <= q_chunk
        sc_ref[pl.ds(off, tk), :] = jnp.where(adm, sc, -jnp.inf)
        smin = jnp.minimum(smin, jnp.min(fold8(jnp.where(adm, sc, jnp.inf)), axis=0))
        smax = jnp.maximum(smax, jnp.max(fold8(jnp.where(adm, sc, -jnp.inf)), axis=0))
        return smin, smax

    smin8, smax8 = lax.fori_loop(0, n_kt, score_tile,
                                 (jnp.full((SUBLANES, qb), jnp.inf, F32),
                                  jnp.full((SUBLANES, qb), -jnp.inf, F32)))
    smin = jnp.min(smin8, axis=0, keepdims=True)
    smax = jnp.max(smax8, axis=0, keepdims=True)

    def count_ge(t):
        def body(kt, c8):
            off = pl.multiple_of(kt * tk, tk)
            return c8 + count8(off, lambda blk, r0: blk >= t)
        c8 = lax.fori_loop(0, n_kt, body, jnp.zeros((SUBLANES, qb), F32))
        return jnp.sum(c8, axis=0, keepdims=True)

    n_adm = (q_chunk + 1) * CHUNK
    take_all = n_adm <= n_sel
    nf = float(n_sel)

    def probe_step(st, t):
        lo, hi, cnt_hi, thr, exact, done_f = st
        active = done_f == 0.0
        c = count_ge(t)
        inside = jnp.logical_and(t > lo, t < hi)
        stuck = jnp.logical_and(active, jnp.logical_not(inside))
        probed = jnp.logical_and(active, inside)
        hit = jnp.logical_and(probed, c == nf)
        move_lo = jnp.logical_and(probed, c > nf)
        move_hi = jnp.logical_and(probed, c < nf)
        finished = jnp.logical_or(hit, stuck)
        return (jnp.where(move_lo, t, lo), jnp.where(move_hi, t, hi), jnp.where(move_hi, c, cnt_hi),
                jnp.where(hit, t, jnp.where(stuck, lo, thr)), jnp.where(hit, 1.0, exact),
                jnp.where(finished, 1.0, done_f))

    def midpoint(lo, hi):
        return lo + (hi - lo) * 0.5

    zeros = jnp.zeros((1, qb), F32)
    take_all_f = jnp.where(take_all, 1.0, 0.0)
    c_top = count_ge(smax)
    top_hit = c_top == nf
    top_ge = c_top >= nf
    live = jnp.logical_not(take_all)
    st = (jnp.where(jnp.logical_and(live, top_ge), smax, smin), smax,
          jnp.where(jnp.logical_and(live, jnp.logical_not(top_ge)), c_top, zeros),
          jnp.where(jnp.logical_and(live, top_hit), smax, jnp.full((1, qb), -jnp.inf, F32)),
          jnp.where(jnp.logical_and(live, top_hit), 1.0, take_all_f),
          jnp.where(jnp.logical_and(live, top_hit), 1.0, take_all_f))
    for special in (0.0, TINY_F32):
        lo, hi = st[0], st[1]
        sp = jnp.full((1, qb), special, F32)
        st = probe_step(st, jnp.where(jnp.logical_and(sp > lo, sp < hi), sp, midpoint(lo, hi)))

    def bis_cond(st):
        return jnp.min(st[5]) < 1.0

    def bis_body(st):
        return probe_step(st, midpoint(st[0], st[1]))

    lo, hi, cnt_hi, thr, exact, _ = lax.while_loop(bis_cond, bis_body, st)
    need = nf - cnt_hi

    def count_tie(jcut):
        def body(kt, c8):
            off = pl.multiple_of(kt * tk, tk)

            def tied_before_cut(blk, r0):
                key = off + r0 + lax.broadcasted_iota(jnp.int32, (blk.shape[0], 1), 0)
                return jnp.logical_and(blk == thr, key <= jcut)
            return c8 + count8(off, tied_before_cut)
        c8 = lax.fori_loop(0, n_kt, body, jnp.zeros((SUBLANES, qb), F32))
        return jnp.sum(c8, axis=0, keepdims=True)

    def tie_cond(st):
        jlo, jhi = st
        return jnp.max(jnp.where(jnp.logical_and(exact == 0.0, jlo < jhi), 1.0, 0.0)) > 0.0

    def tie_body(st):
        jlo, jhi = st
        jmid = jnp.right_shift(jlo + jhi, 1)
        ok = count_tie(jmid) >= need
        return jnp.where(ok, jlo, jmid + 1), jnp.where(ok, jmid, jhi)

    jlo, _ = lax.while_loop(tie_cond, tie_body,
                            (jnp.zeros((1, qb), jnp.int32), jnp.full((1, qb), seq - 1, jnp.int32)))
    jcut = jnp.where(take_all, -1, jnp.where(exact > 0.0, seq, jlo))

    acc_ref[...] = jnp.zeros((N_HEADS * V_ROWS, qb), F32)
    zero_half = jnp.zeros((HEAD_DIM, qb), BF16)
    for p in range(N_HEADS // 2):
        qh_ref[2 * p] = jnp.concatenate([q_ref[0, p * LANES:p * LANES + HEAD_DIM, :], zero_half], axis=0)
        qh_ref[2 * p + 1] = jnp.concatenate([zero_half, q_ref[0, p * LANES + HEAD_DIM:(p + 1) * LANES, :]],
                                            axis=0)

    def masked_logits(kt, dst_ref):
        kt = jnp.minimum(kt, n_kt - 1)
        off = pl.multiple_of(kt * tk, tk)
        blk = sc_ref[pl.ds(off, tk), :]
        sel = jnp.logical_or(blk > thr, jnp.logical_and(blk == thr, key_index(kt) <= jcut))
        bias = jnp.where(sel, 0.0, NEG_BIG).astype(BF16)
        for h in range(N_HEADS):
            p = h // 2
            kp = k_ref[0, pl.ds(off, tk), p * LANES:(p + 1) * LANES]
            dst_ref[h] = _dot(kp, qh_ref[h]).astype(BF16) + bias

    def softmax_pv(kt, src_ref, ms):
        off = pl.multiple_of(kt * tk, tk)
        m_new = []
        for h in range(N_HEADS):
            logits = src_ref[h]
            m_h = jnp.maximum(ms[h], jnp.max(logits, axis=0, keepdims=True).astype(F32))
            alpha = jnp.exp2(ms[h] - m_h)
            probs = jnp.exp2(logits - m_h.astype(BF16))
            vt = vt_ref[0, h * V_ROWS:(h + 1) * V_ROWS, pl.ds(off, tk)]
            rows = slice(h * V_ROWS, (h + 1) * V_ROWS)
            acc_ref[rows, :] = alpha * acc_ref[rows, :] + _dot(vt, probs)
            m_new.append(m_h)
        return tuple(m_new)

    masked_logits(0, lga_ref)

    def tile_pair(i, ms):
        kt0 = 2 * i
        masked_logits(kt0 + 1, lgb_ref)
        ms = softmax_pv(kt0, lga_ref, ms)

        def second(ms):
            masked_logits(kt0 + 2, lga_ref)
            return softmax_pv(kt0 + 1, lgb_ref, ms)

        return lax.cond(kt0 + 1 < n_kt, second, lambda ms: ms, ms)

    m0 = tuple(jnp.full((1, qb), NEG_BIG, F32) for _ in range(N_HEADS))
    lax.fori_loop(0, (n_kt + 1) // 2, tile_pair, m0)
    for h in range(N_HEADS):
        num = acc_ref[h * V_ROWS:h * V_ROWS + HEAD_DIM, :]
        den = acc_ref[h * V_ROWS + HEAD_DIM:h * V_ROWS + HEAD_DIM + 1, :]
        y_ref[0, h * HEAD_DIM:(h + 1) * HEAD_DIM, :] = num / den


def _dsa(ki_ext, k, vt, qi_ext, q, w_t, qb, tk):
    B, S, _ = k.shape
    n_sel = min(TOPK_KEYS_MAX, S // 4)
    resident = lambda shape: pl.BlockSpec(shape, lambda b, j: (b, 0, 0), pipeline_mode=pl.Buffered(1))
    return pl.pallas_call(
        functools.partial(_dsa_kernel, qb=qb, tk=tk, n_sel=n_sel, seq=S),
        grid=(B, S // qb),
        in_specs=[resident((1, S, IDX_K)), resident((1, S, D_ATTN)), resident((1, N_HEADS * V_ROWS, S)),
                  pl.BlockSpec((1, IDX_HEADS, IDX_K, qb), lambda b, j: (b, 0, 0, j)),
                  pl.BlockSpec((1, D_ATTN, qb), lambda b, j: (b, 0, j)),
                  pl.BlockSpec((1, IDX_HEADS, qb), lambda b, j: (b, 0, j))],
        out_specs=pl.BlockSpec((1, D_ATTN, qb), lambda b, j: (b, 0, j)),
        out_shape=jax.ShapeDtypeStruct((B, D_ATTN, S), F32),
        scratch_shapes=[pltpu.VMEM((S, qb), F32), pltpu.VMEM((N_HEADS, LANES, qb), BF16),
                        pltpu.VMEM((N_HEADS * V_ROWS, qb), F32),
                        pltpu.VMEM((N_HEADS, tk, qb), BF16), pltpu.VMEM((N_HEADS, tk, qb), BF16)],
        compiler_params=_cparams("arbitrary", "arbitrary"),
        name="dsa",
    )(ki_ext, k, vt, qi_ext, q, w_t)


def _outproj_kernel(x_ref, yr_ref, ya_ref, mod_ref, ag_ref, wor_ref, woa_ref, n2g_ref,
                    wr_hi_ref, wr_lo_ref, br_ref,
                    x1_ref, h2_ref, meta_ref, cnt_ref, base_ref, *, tm):
    @pl.when(jnp.logical_and(pl.program_id(0) == 0, pl.program_id(1) == 0))
    def _():
        base_ref[...] = jnp.zeros((1, LANES), F32)

    mod = mod_ref[0]
    ya = _rms(ya_ref[0], ag_ref[...]).astype(BF16)
    mix = _dot(yr_ref[0], wor_ref[...]) + _dot(ya, woa_ref[...])
    x1 = x_ref[0] + mod[2:3] * mix
    x1_ref[0] = x1
    h2 = _rms(x1, n2g_ref[...]) * (1.0 + mod[4:5]) + mod[3:4]
    h2_ref[0] = _pack_bf16_pairs(h2)
    h2_hi, h2_lo = _split(h2)
    logits = _dot3(h2_hi, h2_lo, wr_hi_ref[...], wr_lo_ref[...]) + br_ref[...]

    lane = lax.broadcasted_iota(jnp.int32, (tm, LANES), 1).astype(F32)
    work = logits
    vals, onehots, idxs = [], [], []
    for _ in range(TOPK_EXPERTS):
        mx = jnp.max(work, axis=-1, keepdims=True)
        idx = jnp.min(jnp.where(work == mx, lane, float(LANES)), axis=-1, keepdims=True)
        oh = lane == idx
        vals.append(mx)
        idxs.append(idx)
        onehots.append(oh)
        work = jnp.where(oh, -jnp.inf, work)
    es = [jnp.exp(v - vals[0]) for v in vals]
    denom = es[0] + es[1] + es[2] + es[3]
    assigned = jnp.zeros((tm, LANES), F32)
    for oh in onehots:
        assigned = assigned + jnp.where(oh, 1.0, 0.0)
    r_i = lax.broadcasted_iota(jnp.int32, (tm, tm), 0)
    c_i = lax.broadcasted_iota(jnp.int32, (tm, tm), 1)
    ltri = jnp.where(c_i < r_i, 1.0, 0.0).astype(BF16)
    prior = _dot(ltri, assigned.astype(BF16)) + base_ref[...]
    meta = jnp.zeros((tm, LANES), F32)
    for jj in range(TOPK_EXPERTS):
        rank = jnp.sum(jnp.where(onehots[jj], prior, 0.0), axis=-1, keepdims=True)
        meta = jnp.where(lane == jj, idxs[jj], meta)
        meta = jnp.where(lane == TOPK_EXPERTS + jj, es[jj] / denom, meta)
        meta = jnp.where(lane == 2 * TOPK_EXPERTS + jj, rank, meta)
    meta_ref[0] = meta
    base = base_ref[...] + jnp.sum(assigned, axis=0, keepdims=True)
    base_ref[...] = base
    cnt_ref[...] = base


def _outproj(x, y_rnn, y_attn, mod6, attn_out_g, w_out, norm2_g, w_router, b_router, tm):
    B, S, D = x.shape
    wor = w_out[:D_RNN].astype(BF16)
    woa = w_out[D_RNN:].astype(BF16)
    wr = jnp.zeros((D, LANES), F32).at[:, :N_EXPERTS].set(w_router)
    wr_hi, wr_lo = _split(wr)
    br = jnp.full((1, LANES), NEG_BIG, F32).at[0, :N_EXPERTS].set(b_router)
    const = lambda shape: pl.BlockSpec(shape, lambda b, s: (0,) * len(shape))
    row = lambda n: pl.BlockSpec((1, tm, n), lambda b, s: (b, s, 0))
    return pl.pallas_call(
        functools.partial(_outproj_kernel, tm=tm),
        grid=(B, S // tm),
        in_specs=[row(D), row(D_RNN), row(D_ATTN), pl.BlockSpec((1, 6, D), lambda b, s: (b, 0, 0)),
                  const((1, D_ATTN)), const((D_RNN, D)), const((D_ATTN, D)), const((1, D)),
                  const((D, LANES)), const((D, LANES)), const((1, LANES))],
        out_specs=[row(D), row(D // 2), row(LANES), const((1, LANES))],
        out_shape=[jax.ShapeDtypeStruct((B, S, D), F32), jax.ShapeDtypeStruct((B, S, D // 2), jnp.int32),
                   jax.ShapeDtypeStruct((B, S, LANES), F32), jax.ShapeDtypeStruct((1, LANES), F32)],
        scratch_shapes=[pltpu.VMEM((1, LANES), F32)],
        compiler_params=_cparams("arbitrary", "arbitrary"),
        name="outproj",
    )(x, y_rnn, y_attn, mod6, attn_out_g.reshape(1, D_ATTN), wor, woa, norm2_g.reshape(1, D),
      wr_hi, wr_lo, br)


GLU_GROUP = 2 * LANES


def _degroup_kernel(w_ref, p_ref, o_ref):
    p = p_ref[...]
    for g in range(2 * D_FF // GLU_GROUP):
        cols = slice(g * GLU_GROUP, (g + 1) * GLU_GROUP)
        o_ref[0, :, cols] = _dot(w_ref[0, :, cols].astype(BF16), p).astype(BF16)


def _degroup_w1(w1):
    E, D, F2 = w1.shape
    src = jnp.arange(GLU_GROUP)
    dst = jnp.where(src % 2 == 0, src // 2, LANES + src // 2)
    perm = (dst[:, None] == jnp.arange(GLU_GROUP)[None, :]).astype(BF16)
    tr = 512
    return pl.pallas_call(
        _degroup_kernel,
        grid=(E, D // tr),
        in_specs=[pl.BlockSpec((1, tr, F2), lambda e, r: (e, r, 0)),
                  pl.BlockSpec((GLU_GROUP, GLU_GROUP), lambda e, r: (0, 0))],
        out_specs=pl.BlockSpec((1, tr, F2), lambda e, r: (e, r, 0)),
        out_shape=jax.ShapeDtypeStruct((E, D, F2), BF16),
        compiler_params=_cparams("arbitrary", "arbitrary"),
        name="degroup_w1",
    )(w1, perm)


def _expert_kernel(be_ref, nb_ref, x_ref, w1_ref, b1_ref, w2_ref, b2_ref, y_ref):
    del be_ref

    @pl.when(pl.program_id(0) < nb_ref[0])
    def _():
        x = _unpack_bf16_pairs(x_ref[...]).astype(BF16)
        u = _dot(x, w1_ref[0]) + b1_ref[0]
        acts = []
        for g in range(2 * D_FF // GLU_GROUP):
            ug = jnp.minimum(u[:, g * GLU_GROUP:g * GLU_GROUP + LANES], SWIGLU_LIMIT)
            ul = jnp.clip(u[:, g * GLU_GROUP + LANES:(g + 1) * GLU_GROUP], -SWIGLU_LIMIT, SWIGLU_LIMIT)
            acts.append((ug * jax.nn.sigmoid(SWIGLU_ALPHA * ug) * (ul + 1.0)).astype(BF16))
        act = jnp.concatenate(acts, axis=1)
        y_ref[...] = _pack_bf16_pairs(_dot(act, w2_ref[0]) + b2_ref[0])

    @pl.when(pl.program_id(0) >= nb_ref[0])
    def _():
        y_ref[...] = jnp.zeros(y_ref.shape, y_ref.dtype)


def _experts(x_sorted, block_e, n_used, w1, b1, w2, b2, mb):
    n_rows, half = x_sorted.shape
    D = 2 * half
    E = w1.shape[0]
    w1p = _degroup_w1(w1)
    ng = 2 * D_FF // GLU_GROUP
    b1p = jnp.swapaxes(b1.reshape(E, ng, LANES, 2), 2, 3).reshape(E, 1, 2 * D_FF)
    wspec = lambda k, n: pl.BlockSpec((1, k, n), lambda i, be, nb: (be[i], 0, 0))
    grid_spec = pltpu.PrefetchScalarGridSpec(
        num_scalar_prefetch=2,
        grid=(n_rows // mb,),
        in_specs=[pl.BlockSpec((mb, half), lambda i, be, nb: (jnp.minimum(i, nb[0] - 1), 0)),
                  wspec(D, 2 * D_FF), wspec(1, 2 * D_FF), wspec(D_FF, D), wspec(1, D)],
        out_specs=pl.BlockSpec((mb, half), lambda i, be, nb: (i, 0)),
    )
    return pl.pallas_call(
        _expert_kernel,
        grid_spec=grid_spec,
        out_shape=jax.ShapeDtypeStruct((n_rows, half), jnp.int32),
        compiler_params=_cparams("arbitrary"),
        name="experts",
    )(block_e, n_used, x_sorted, w1p, b1p, w2.astype(BF16), b2.reshape(E, 1, D))


def _combine_kernel(x1_ref, yg_ref, meta_ref, mod_ref, o_ref):
    meta = meta_ref[0]
    ff = meta[:, TOPK_EXPERTS:TOPK_EXPERTS + 1] * _unpack_bf16_pairs(yg_ref[0])
    for jj in range(1, TOPK_EXPERTS):
        ff = ff + meta[:, TOPK_EXPERTS + jj:TOPK_EXPERTS + jj + 1] * _unpack_bf16_pairs(yg_ref[jj])
    o_ref[0] = x1_ref[0] + mod_ref[0][5:6] * ff


def _combine(x1, yg, meta, mod6, tm):
    B, S, D = x1.shape
    nt = S // tm
    return pl.pallas_call(
        _combine_kernel,
        grid=(B, nt),
        in_specs=[pl.BlockSpec((1, tm, D), lambda b, s: (b, s, 0)),
                  pl.BlockSpec((TOPK_EXPERTS, tm, D // 2), lambda b, s: (0, b * nt + s, 0)),
                  pl.BlockSpec((1, tm, LANES), lambda b, s: (b, s, 0)),
                  pl.BlockSpec((1, 6, D), lambda b, s: (b, 0, 0))],
        out_specs=pl.BlockSpec((1, tm, D), lambda b, s: (b, s, 0)),
        out_shape=jax.ShapeDtypeStruct((B, S, D), F32),
        compiler_params=_cparams("arbitrary", "arbitrary"),
        name="combine",
    )(x1, yg, meta, mod6)


def _tile(n, pref):
    t = pref
    while n % t:
        t //= 2
    return t


def _layer(x, c, w_ada, b_ada, norm1_g, w_in, conv_w, conv_b, w_rg_a, b_rg_a, w_rg_x, b_rg_x,
           lru_lambda, q_norm_g, k_norm_g, kidx_norm_g, rg_out_g, attn_out_g, w_out, norm2_g,
           w_router, b_router, w1, b1, w2, b2):
    B, S, D = x.shape
    T = B * S
    tm = _tile(S, 512)
    mod6 = _ada(c, w_ada, b_ada).reshape(B, 6, D)

    xr, xg, q, k, v, qi, kiw = _inproj(x, mod6, norm1_g, w_in, q_norm_g, k_norm_g, kidx_norm_g, tm)
    y_rnn = _rglru(xr, xg, conv_w, conv_b, w_rg_a, b_rg_a, w_rg_x, b_rg_x, lru_lambda, rg_out_g,
                   _tile(S, 256))

    ki = kiw[..., :IDX_DIM]
    ki_hi, ki_lo = _split(ki)
    ki_ext = jnp.concatenate([ki_hi, ki_hi, ki_lo, jnp.zeros_like(ki_hi)], axis=-1)
    qi_t = jnp.transpose(qi.reshape(B, S, IDX_HEADS, IDX_DIM), (0, 2, 3, 1))
    qi_hi, qi_lo = _split(qi_t)
    qi_ext = jnp.concatenate([qi_hi, qi_lo, qi_hi, jnp.zeros_like(qi_hi)], axis=2)
    q_t = jnp.swapaxes(q, 1, 2)
    w_t = jnp.swapaxes(kiw[..., IDX_DIM:IDX_DIM + IDX_HEADS], 1, 2)
    v_h = v.reshape(B, S, N_HEADS, HEAD_DIM)
    v_ext = jnp.concatenate([v_h, jnp.ones((B, S, N_HEADS, 1), BF16),
                             jnp.zeros((B, S, N_HEADS, V_ROWS - HEAD_DIM - 1), BF16)], axis=-1)
    vt = jnp.swapaxes(v_ext.reshape(B, S, N_HEADS * V_ROWS), 1, 2)
    qb = _tile(S, 256)
    y_attn_t = _dsa(ki_ext, k, vt, qi_ext, q_t, w_t, qb, _tile(S, 512))
    y_attn = jnp.swapaxes(y_attn_t, 1, 2)

    x1, h2, meta, cnt = _outproj(x, y_rnn, y_attn, mod6, attn_out_g, w_out, norm2_g,
                                 w_router, b_router, tm)

    mb = 512
    top_e = meta[..., 0:TOPK_EXPERTS].astype(jnp.int32).reshape(T, TOPK_EXPERTS)
    rank = meta[..., 2 * TOPK_EXPERTS:3 * TOPK_EXPERTS].astype(jnp.int32).reshape(T, TOPK_EXPERTS)
    counts = cnt[0, :N_EXPERTS].astype(jnp.int32)
    padded = (counts + mb - 1) // mb * mb
    pad_end = jnp.cumsum(padded)
    pad_start = pad_end - padded
    dest = pad_start[top_e] + rank
    n_rows = T * TOPK_EXPERTS + N_EXPERTS * mb
    n_blocks = n_rows // mb
    block_start = jnp.arange(n_blocks, dtype=jnp.int32) * mb
    block_e = jnp.minimum(jnp.sum((pad_end[None, :] <= block_start[:, None]).astype(jnp.int32), axis=1),
                          N_EXPERTS - 1)
    tok = jnp.broadcast_to(jnp.arange(T, dtype=jnp.int32)[:, None], (T, TOPK_EXPERTS))
    _, sorted_tok = lax.sort_key_val(dest.reshape(-1), tok.reshape(-1))
    group_start = jnp.cumsum(counts) - counts
    pad_before = jnp.repeat((pad_start - group_start)[block_e], mb)
    pos = jnp.clip(jnp.arange(n_rows, dtype=jnp.int32) - pad_before, 0, T * TOPK_EXPERTS - 1)
    row_tok = sorted_tok[pos]
    x_sorted = h2.reshape(T, D // 2)[row_tok]
    n_used = (pad_end[N_EXPERTS - 1:] // mb).astype(jnp.int32)
    y_sorted = _experts(x_sorted, block_e, n_used, w1, b1, w2, b2, mb)
    yg = y_sorted[dest.T]
    return _combine(x1, yg, meta, mod6, tm)


def kernel(x, c, w_ada, b_ada, norm1_g, w_in, conv_w, conv_b, w_rg_a, b_rg_a, w_rg_x, b_rg_x,
           lru_lambda, q_norm_g, k_norm_g, kidx_norm_g, rg_out_g, attn_out_g, w_out, norm2_g,
           w_router, b_router, w1, b1, w2, b2):
    depth = w_ada.shape[0]
    for l in range(depth):
        x = _layer(x, c, w_ada[l], b_ada[l], norm1_g[l], w_in[l], conv_w[l], conv_b[l],
                   w_rg_a[l], b_rg_a[l], w_rg_x[l], b_rg_x[l], lru_lambda[l], q_norm_g[l],
                   k_norm_g[l], kidx_norm_g[l], rg_out_g[l], attn_out_g[l], w_out[l], norm2_g[l],
                   w_router[l], b_router[l], w1[l], b1[l], w2[l], b2[l])
    return x
```

```python
import functools

import jax
import jax.numpy as jnp
from jax import lax
from jax.experimental import pallas as pl
from jax.experimental.pallas import tpu as pltpu

D_MODEL = 1024
CHUNK = 64
CHUNK_SHIFT = 6
D_RNN = 512
RNN_BLOCKS = 8
RNN_BLOCK = 64
CONV_WIDTH = 4
RG_C = 8.0
N_HEADS = 8
HEAD_DIM = 64
D_ATTN = 512
IDX_HEADS = 8
IDX_DIM = 64
TOPK_KEYS_MAX = 256
N_EXPERTS = 32
TOPK_EXPERTS = 4
D_FF = 1024
SWIGLU_ALPHA = 1.702
SWIGLU_LIMIT = 7.0
EPS = 1e-6

LANES = 128
SUBLANES = 8
TINY_F32 = 1.1754943508222875e-38
REDUCE_CHAINS = 8
IDX_K = IDX_DIM
NEG_BIG = -(2.0 ** 100)
LOG2E = 1.4426950408889634
V_ROWS = 80
VMEM_LIMIT = 56 * 1024 * 1024

F32 = jnp.float32
BF16 = jnp.bfloat16


def _dot(a, b):
    return jnp.dot(a, b, preferred_element_type=F32)


def _dot_nt(a, b):
    return lax.dot_general(a, b, (((1,), (1,)), ((), ())), preferred_element_type=F32)


def _split(a):
    hi = a.astype(BF16)
    lo = (a - hi.astype(F32)).astype(BF16)
    return hi, lo


def _dot3(a_hi, a_lo, b_hi, b_lo):
    return _dot(a_hi, b_hi) + (_dot(a_hi, b_lo) + _dot(a_lo, b_hi))


def _pack_bf16_pairs(a):
    n = a.shape[1] // 2
    hi = lax.bitcast_convert_type(a[:, :n].astype(BF16).astype(F32), jnp.int32)
    lo = lax.bitcast_convert_type(a[:, n:].astype(BF16).astype(F32), jnp.int32)
    return jnp.bitwise_or(hi, lax.shift_right_logical(lo, 16))


def _unpack_bf16_pairs(u):
    hi = lax.bitcast_convert_type(jnp.bitwise_and(u, -65536), F32)
    lo = lax.bitcast_convert_type(lax.shift_left(u, 16), F32)
    return jnp.concatenate([hi, lo], axis=1)


def _rms(x, g):
    ms = jnp.mean(x * x, axis=-1, keepdims=True)
    return x * lax.rsqrt(ms + EPS) * g


def _cparams(*sem):
    return pltpu.CompilerParams(dimension_semantics=sem, vmem_limit_bytes=VMEM_LIMIT)


def _ada_kernel(c_ref, w_ref, b_ref, o_ref):
    c = c_ref[...]
    s = c * jax.nn.sigmoid(c)
    s_hi, s_lo = _split(s)
    w_hi, w_lo = _split(w_ref[...])
    o_ref[...] = _dot3(s_hi, s_lo, w_hi, w_lo) + b_ref[...]


def _ada(c, w_ada, b_ada):
    B, D = c.shape
    N = w_ada.shape[1]
    rows = 16
    c_pad = jnp.zeros((rows, D), F32).at[:B].set(c)
    tn = 1024
    out = pl.pallas_call(
        _ada_kernel,
        grid=(N // tn,),
        in_specs=[pl.BlockSpec((rows, D), lambda j: (0, 0)),
                  pl.BlockSpec((D, tn), lambda j: (0, j)),
                  pl.BlockSpec((1, tn), lambda j: (0, j))],
        out_specs=pl.BlockSpec((rows, tn), lambda j: (0, j)),
        out_shape=jax.ShapeDtypeStruct((rows, N), F32),
        compiler_params=_cparams("arbitrary"),
        name="ada",
    )(c_pad, w_ada, b_ada.reshape(1, N))
    return out[:B]


def _inproj_kernel(x_ref, mod_ref, g1_ref, wmain_ref, widx_ref, bd_ref,
                   qg_ref, kg_ref, kig_ref,
                   xr_ref, xg_ref, q_ref, k_ref, v_ref, qi_ref, kiw_ref):
    x = x_ref[0]
    mod = mod_ref[0]
    h = _rms(x, g1_ref[...]) * (1.0 + mod[1:2]) + mod[0:1]
    h_hi = h.astype(BF16)
    bd = bd_ref[...]

    def head_norm(z, g):
        sq_hi, sq_lo = _split(z * z)
        ms = _dot(sq_hi, bd) + _dot(sq_lo, bd)
        return z * lax.rsqrt(ms + EPS) * g

    xr_ref[0] = _dot(h_hi, wmain_ref[:, 0:512])
    xg_ref[0] = _dot(h_hi, wmain_ref[:, 512:1024])
    zq = _dot(h_hi, wmain_ref[:, 1024:1536])
    q_ref[0] = (head_norm(zq, qg_ref[...]) * (HEAD_DIM ** -0.5 * LOG2E)).astype(BF16)
    zk = _dot(h_hi, wmain_ref[:, 1536:2048])
    k_ref[0] = head_norm(zk, kg_ref[...]).astype(BF16)
    v_ref[0] = _dot(h_hi, wmain_ref[:, 2048:2560]).astype(BF16)

    zi = _dot(h_hi, widx_ref[...])
    qi_ref[0] = (zi[:, 0:512] * (IDX_DIM ** -0.5)).astype(BF16)
    kw = zi[:, 512:640]
    lane = lax.broadcasted_iota(jnp.int32, kw.shape, 1)
    is_ki = lane < IDX_DIM
    ms = jnp.sum(jnp.where(is_ki, kw * kw, 0.0), axis=-1, keepdims=True) * (1.0 / IDX_DIM)
    kiw_ref[0] = jnp.where(is_ki, kw * lax.rsqrt(ms + EPS) * kig_ref[...], kw * (IDX_HEADS ** -0.5))


def _inproj(x, mod6, norm1_g, w_in, q_norm_g, k_norm_g, kidx_norm_g, tm):
    B, S, D = x.shape
    wmain = w_in[:, :2560].astype(BF16)
    widx = jnp.concatenate([w_in[:, 2560:3144], jnp.zeros((D, 640 - 584), F32)], axis=1).astype(BF16)
    head = jnp.arange(D_ATTN) // HEAD_DIM
    bd = jnp.where(head[:, None] == head[None, :], 1.0 / HEAD_DIM, 0.0).astype(BF16)
    qg = jnp.tile(q_norm_g, N_HEADS).reshape(1, D_ATTN)
    kg = jnp.tile(k_norm_g, N_HEADS).reshape(1, D_ATTN)
    kig = jnp.concatenate([kidx_norm_g, jnp.zeros((LANES - IDX_DIM,), F32)]).reshape(1, LANES)
    const = lambda shape: pl.BlockSpec(shape, lambda b, s: (0,) * len(shape))
    row = lambda n: pl.BlockSpec((1, tm, n), lambda b, s: (b, s, 0))
    return pl.pallas_call(
        _inproj_kernel,
        grid=(B, S // tm),
        in_specs=[row(D), pl.BlockSpec((1, 6, D), lambda b, s: (b, 0, 0)), const((1, D)),
                  const((D, 2560)), const((D, 640)), const((D_ATTN, D_ATTN)),
                  const((1, D_ATTN)), const((1, D_ATTN)), const((1, LANES))],
        out_specs=[row(512), row(512), row(512), row(512), row(512), row(512), row(LANES)],
        out_shape=[jax.ShapeDtypeStruct((B, S, 512), F32), jax.ShapeDtypeStruct((B, S, 512), F32),
                   jax.ShapeDtypeStruct((B, S, 512), BF16), jax.ShapeDtypeStruct((B, S, 512), BF16),
                   jax.ShapeDtypeStruct((B, S, 512), BF16), jax.ShapeDtypeStruct((B, S, 512), BF16),
                   jax.ShapeDtypeStruct((B, S, LANES), F32)],
        compiler_params=_cparams("arbitrary", "arbitrary"),
        name="inproj",
    )(x, mod6, norm1_g.reshape(1, D), wmain, widx, bd, qg, kg, kig)


def _rglru_kernel(xr_ref, xg_ref, cw_ref, cb_ref, wa_hi_ref, wa_lo_ref, wx_hi_ref, wx_lo_ref,
                  ba_ref, bx_ref, lam_ref, g_ref, y_ref, ext_ref, hlast_ref, *, ts):
    @pl.when(pl.program_id(1) == 0)
    def _():
        ext_ref[0:8, :] = jnp.zeros((8, D_RNN), F32)
        hlast_ref[...] = jnp.zeros((1, D_RNN), F32)

    xr = xr_ref[0]
    ext_ref[8:8 + ts, :] = xr
    cw = cw_ref[...]
    xc = cb_ref[...] + cw[0:1] * ext_ref[5:5 + ts, :]
    for j in range(1, CONV_WIDTH):
        xc = xc + cw[j:j + 1] * ext_ref[5 + j:5 + j + ts, :]
    ext_ref[0:8, :] = xr[ts - 8:ts]

    xc_hi, xc_lo = _split(xc)
    r = jax.nn.sigmoid(_dot3(xc_hi, xc_lo, wa_hi_ref[...], wa_lo_ref[...]) + ba_ref[...])
    i = jax.nn.sigmoid(_dot3(xc_hi, xc_lo, wx_hi_ref[...], wx_lo_ref[...]) + bx_ref[...])
    lam = lam_ref[...]
    sp = jnp.maximum(-lam, 0.0) + jnp.log1p(jnp.exp(-jnp.abs(lam)))
    log_a = (-RG_C) * r * sp
    a = jnp.exp(log_a)
    mult = jnp.sqrt(-jnp.tanh(log_a) * (a * a + 1.0))
    u = mult * i * xc

    row = lax.broadcasted_iota(jnp.int32, (ts, D_RNN), 0)
    acc_a, acc_h = a, u
    d = 1
    while d < ts:
        sh_a = pltpu.roll(acc_a, d, 0)
        sh_h = pltpu.roll(acc_h, d, 0)
        ok = row >= d
        acc_h = jnp.where(ok, acc_a * sh_h + acc_h, acc_h)
        acc_a = jnp.where(ok, acc_a * sh_a, acc_a)
        d *= 2
    hseq = acc_h + acc_a * hlast_ref[...]
    hlast_ref[...] = hseq[ts - 1:ts]

    xg = xg_ref[0]
    gelu = 0.5 * xg * (1.0 + jnp.tanh(0.7978845608028654 * (xg + 0.044715 * (xg * xg * xg))))
    y_ref[0] = _rms(gelu * hseq, g_ref[...]).astype(BF16)


def _block_diag(w):
    H, n, _ = w.shape
    eye = jnp.eye(H, dtype=w.dtype)
    return (eye[:, None, :, None] * w[:, :, None, :]).reshape(H * n, H * n)


def _rglru(xr, xg, conv_w, conv_b, w_a, b_a, w_x, b_x, lam, g, ts):
    B, S, _ = xr.shape
    wa_hi, wa_lo = _split(_block_diag(w_a))
    wx_hi, wx_lo = _split(_block_diag(w_x))
    const = lambda shape: pl.BlockSpec(shape, lambda b, s: (0,) * len(shape))
    row = pl.BlockSpec((1, ts, D_RNN), lambda b, s: (b, s, 0))
    vec = lambda a: a.reshape(1, D_RNN)
    return pl.pallas_call(
        functools.partial(_rglru_kernel, ts=ts),
        grid=(B, S // ts),
        in_specs=[row, row, const((CONV_WIDTH, D_RNN)), const((1, D_RNN)),
                  const((D_RNN, D_RNN)), const((D_RNN, D_RNN)), const((D_RNN, D_RNN)), const((D_RNN, D_RNN)),
                  const((1, D_RNN)), const((1, D_RNN)), const((1, D_RNN)), const((1, D_RNN))],
        out_specs=row,
        out_shape=jax.ShapeDtypeStruct((B, S, D_RNN), BF16),
        scratch_shapes=[pltpu.VMEM((ts + 8, D_RNN), F32), pltpu.VMEM((1, D_RNN), F32)],
        compiler_params=_cparams("arbitrary", "arbitrary"),
        name="rglru",
    )(xr, xg, conv_w, vec(conv_b), wa_hi, wa_lo, wx_hi, wx_lo, vec(b_a), vec(b_x), vec(lam), vec(g))


def _dsa_kernel(ki_ref, k_ref, vt_ref, qi_ref, q_ref, w_ref, y_ref,
                sc_ref, qh_ref, acc_ref, lga_ref, lgb_ref, *, qb, tk, n_sel, seq):
    j = pl.program_id(1)
    n_kt = ((j + 1) * qb + tk - 1) // tk
    q_idx = j * qb + lax.broadcasted_iota(jnp.int32, (1, qb), 1)
    q_chunk = jnp.right_shift(q_idx, CHUNK_SHIFT)

    def key_index(kt):
        return kt * tk + lax.broadcasted_iota(jnp.int32, (tk, 1), 0)

    w = w_ref[0]

    def fold8(a):
        return a.reshape(tk // SUBLANES, SUBLANES, qb)

    def count8(off, pred):
        rows = tk // REDUCE_CHAINS
        parts = []
        for i in range(REDUCE_CHAINS):
            blk = sc_ref[pl.ds(off + i * rows, rows), :]
            ind = jnp.where(pred(blk, i * rows), 1.0, 0.0)
            parts.append(jnp.sum(ind.reshape(rows // SUBLANES, SUBLANES, qb), axis=0))
        while len(parts) > 1:
            parts = [parts[i] + parts[i + 1] for i in range(0, len(parts), 2)]
        return parts[0]

    def score_tile(kt, carry):
        smin, smax = carry
        off = pl.multiple_of(kt * tk, tk)
        ki = ki_ref[0, pl.ds(off, tk), :]
        sc = jnp.zeros((tk, qb), F32)
        for h in range(IDX_HEADS):
            s = _dot(ki, qi_ref[0, h])
            sc = sc + w[h:h + 1] * jnp.maximum(s, 0.0)
        adm = jnp.right_shift(key_index(kt), CHUNK_SHIFT) <= q_chunk
        sc_ref[pl.ds(off, tk), :] = jnp.where(adm, sc, -jnp.inf)
        smin = jnp.minimum(smin, jnp.min(fold8(jnp.where(adm, sc, jnp.inf)), axis=0))
        smax = jnp.maximum(smax, jnp.max(fold8(jnp.where(adm, sc, -jnp.inf)), axis=0))
        return smin, smax

    smin8, smax8 = lax.fori_loop(0, n_kt, score_tile,
                                 (jnp.full((SUBLANES, qb), jnp.inf, F32),
                                  jnp.full((SUBLANES, qb), -jnp.inf, F32)))
    smin = jnp.min(smin8, axis=0, keepdims=True)
    smax = jnp.max(smax8, axis=0, keepdims=True)

    def count_ge(t):
        def body(kt, c8):
            off = pl.multiple_of(kt * tk, tk)
            return c8 + count8(off, lambda blk, r0: blk >= t)
        c8 = lax.fori_loop(0, n_kt, body, jnp.zeros((SUBLANES, qb), F32))
        return jnp.sum(c8, axis=0, keepdims=True)

    n_adm = (q_chunk + 1) * CHUNK
    take_all = n_adm <= n_sel
    nf = float(n_sel)

    def probe_step(st, t):
        lo, hi, cnt_hi, thr, exact, done_f = st
        active = done_f == 0.0
        c = count_ge(t)
        inside = jnp.logical_and(t > lo, t < hi)
        stuck = jnp.logical_and(active, jnp.logical_not(inside))
        probed = jnp.logical_and(active, inside)
        hit = jnp.logical_and(probed, c == nf)
        move_lo = jnp.logical_and(probed, c > nf)
        move_hi = jnp.logical_and(probed, c < nf)
        finished = jnp.logical_or(hit, stuck)
        return (jnp.where(move_lo, t, lo), jnp.where(move_hi, t, hi), jnp.where(move_hi, c, cnt_hi),
                jnp.where(hit, t, jnp.where(stuck, lo, thr)), jnp.where(hit, 1.0, exact),
                jnp.where(finished, 1.0, done_f))

    def midpoint(lo, hi):
        return lo + (hi - lo) * 0.5

    zeros = jnp.zeros((1, qb), F32)
    take_all_f = jnp.where(take_all, 1.0, 0.0)
    c_top = count_ge(smax)
    top_hit = c_top == nf
    top_ge = c_top >= nf
    live = jnp.logical_not(take_all)
    st = (jnp.where(jnp.logical_and(live, top_ge), smax, smin), smax,
          jnp.where(jnp.logical_and(live, jnp.logical_not(top_ge)), c_top, zeros),
          jnp.where(jnp.logical_and(live, top_hit), smax, jnp.full((1, qb), -jnp.inf, F32)),
          jnp.where(jnp.logical_and(live, top_hit), 1.0, take_all_f),
          jnp.where(jnp.logical_and(live, top_hit), 1.0, take_all_f))
    for special in (0.0, TINY_F32):
        lo, hi = st[0], st[1]
        sp = jnp.full((1, qb), special, F32)
        st = probe_step(st, jnp.where(jnp.logical_and(sp > lo, sp < hi), sp, midpoint(lo, hi)))

    def bis_cond(st):
        return jnp.min(st[5]) < 1.0

    def bis_body(st):
        return probe_step(st, midpoint(st[0], st[1]))

    lo, hi, cnt_hi, thr, exact, _ = lax.while_loop(bis_cond, bis_body, st)
    need = nf - cnt_hi

    def count_tie(jcut):
        def body(kt, c8):
            off = pl.multiple_of(kt * tk, tk)

            def tied_before_cut(blk, r0):
                key = off + r0 + lax.broadcasted_iota(jnp.int32, (blk.shape[0], 1), 0)
                return jnp.logical_and(blk == thr, key <= jcut)
            return c8 + count8(off, tied_before_cut)
        c8 = lax.fori_loop(0, n_kt, body, jnp.zeros((SUBLANES, qb), F32))
        return jnp.sum(c8, axis=0, keepdims=True)

    def tie_cond(st):
        jlo, jhi = st
        return jnp.max(jnp.where(jnp.logical_and(exact == 0.0, jlo < jhi), 1.0, 0.0)) > 0.0

    def tie_body(st):
        jlo, jhi = st
        jmid = jnp.right_shift(jlo + jhi, 1)
        ok = count_tie(jmid) >= need
        return jnp.where(ok, jlo, jmid + 1), jnp.where(ok, jmid, jhi)

    jlo, _ = lax.while_loop(tie_cond, tie_body,
                            (jnp.zeros((1, qb), jnp.int32), jnp.full((1, qb), seq - 1, jnp.int32)))
    jcut = jnp.where(take_all, -1, jnp.where(exact > 0.0, seq, jlo))

    acc_ref[...] = jnp.zeros((N_HEADS * V_ROWS, qb), F32)
    zero_half = jnp.zeros((HEAD_DIM, qb), BF16)
    for p in range(N_HEADS // 2):
        qh_ref[2 * p] = jnp.concatenate([q_ref[0, p * LANES:p * LANES + HEAD_DIM, :], zero_half], axis=0)
        qh_ref[2 * p + 1] = jnp.concatenate([zero_half, q_ref[0, p * LANES + HEAD_DIM:(p + 1) * LANES, :]],
                                            axis=0)

    def masked_logits(kt, dst_ref):
        kt = jnp.minimum(kt, n_kt - 1)
        off = pl.multiple_of(kt * tk, tk)
        blk = sc_ref[pl.ds(off, tk), :]
        sel = jnp.logical_or(blk > thr, jnp.logical_and(blk == thr, key_index(kt) <= jcut))
        bias = jnp.where(sel, 0.0, NEG_BIG).astype(BF16)
        for h in range(N_HEADS):
            p = h // 2
            kp = k_ref[0, pl.ds(off, tk), p * LANES:(p + 1) * LANES]
            dst_ref[h] = _dot(kp, qh_ref[h]).astype(BF16) + bias

    def softmax_pv(kt, src_ref, ms):
        off = pl.multiple_of(kt * tk, tk)
        m_new = []
        for h in range(N_HEADS):
            logits = src_ref[h]
            m_h = jnp.maximum(ms[h], jnp.max(logits, axis=0, keepdims=True).astype(F32))
            alpha = jnp.exp2(ms[h] - m_h)
            probs = jnp.exp2(logits - m_h.astype(BF16))
            vt = vt_ref[0, h * V_ROWS:(h + 1) * V_ROWS, pl.ds(off, tk)]
            rows = slice(h * V_ROWS, (h + 1) * V_ROWS)
            acc_ref[rows, :] = alpha * acc_ref[rows, :] + _dot(vt, probs)
            m_new.append(m_h)
        return tuple(m_new)

    masked_logits(0, lga_ref)

    def tile_pair(i, ms):
        kt0 = 2 * i
        masked_logits(kt0 + 1, lgb_ref)
        ms = softmax_pv(kt0, lga_ref, ms)

        def second(ms):
            masked_logits(kt0 + 2, lga_ref)
            return softmax_pv(kt0 + 1, lgb_ref, ms)

        return lax.cond(kt0 + 1 < n_kt, second, lambda ms: ms, ms)

    m0 = tuple(jnp.full((1, qb), NEG_BIG, F32) for _ in range(N_HEADS))
    lax.fori_loop(0, (n_kt + 1) // 2, tile_pair, m0)
    for h in range(N_HEADS):
        num = acc_ref[h * V_ROWS:h * V_ROWS + HEAD_DIM, :]
        den = acc_ref[h * V_ROWS + HEAD_DIM:h * V_ROWS + HEAD_DIM + 1, :]
        y_ref[0, h * HEAD_DIM:(h + 1) * HEAD_DIM, :] = num / den


def _dsa(ki_ext, k, vt, qi_ext, q, w_t, qb, tk):
    B, S, _ = k.shape
    n_sel = min(TOPK_KEYS_MAX, S // 4)
    resident = lambda shape: pl.BlockSpec(shape, lambda b, j: (b, 0, 0), pipeline_mode=pl.Buffered(1))
    return pl.pallas_call(
        functools.partial(_dsa_kernel, qb=qb, tk=tk, n_sel=n_sel, seq=S),
        grid=(B, S // qb),
        in_specs=[resident((1, S, IDX_K)), resident((1, S, D_ATTN)), resident((1, N_HEADS * V_ROWS, S)),
                  pl.BlockSpec((1, IDX_HEADS, IDX_K, qb), lambda b, j: (b, 0, 0, j)),
                  pl.BlockSpec((1, D_ATTN, qb), lambda b, j: (b, 0, j)),
                  pl.BlockSpec((1, IDX_HEADS, qb), lambda b, j: (b, 0, j))],
        out_specs=pl.BlockSpec((1, D_ATTN, qb), lambda b, j: (b, 0, j)),
        out_shape=jax.ShapeDtypeStruct((B, D_ATTN, S), F32),
        scratch_shapes=[pltpu.VMEM((S, qb), F32), pltpu.VMEM((N_HEADS, LANES, qb), BF16),
                        pltpu.VMEM((N_HEADS * V_ROWS, qb), F32),
                        pltpu.VMEM((N_HEADS, tk, qb), BF16), pltpu.VMEM((N_HEADS, tk, qb), BF16)],
        compiler_params=_cparams("arbitrary", "arbitrary"),
        name="dsa",
    )(ki_ext, k, vt, qi_ext, q, w_t)


def _outproj_kernel(x_ref, yr_ref, ya_ref, mod_ref, ag_ref, wor_ref, woa_ref, n2g_ref,
                    wr_hi_ref, wr_lo_ref, br_ref,
                    x1_ref, h2_ref, meta_ref, cnt_ref, base_ref, *, tm):
    @pl.when(jnp.logical_and(pl.program_id(0) == 0, pl.program_id(1) == 0))
    def _():
        base_ref[...] = jnp.zeros((1, LANES), F32)

    mod = mod_ref[0]
    ya = _rms(ya_ref[0], ag_ref[...]).astype(BF16)
    mix = _dot(yr_ref[0], wor_ref[...]) + _dot(ya, woa_ref[...])
    x1 = x_ref[0] + mod[2:3] * mix
    x1_ref[0] = x1
    h2 = _rms(x1, n2g_ref[...]) * (1.0 + mod[4:5]) + mod[3:4]
    h2_ref[0] = _pack_bf16_pairs(h2)
    h2_hi, h2_lo = _split(h2)
    logits = _dot3(h2_hi, h2_lo, wr_hi_ref[...], wr_lo_ref[...]) + br_ref[...]

    lane = lax.broadcasted_iota(jnp.int32, (tm, LANES), 1).astype(F32)
    work = logits
    vals, onehots, idxs = [], [], []
    for _ in range(TOPK_EXPERTS):
        mx = jnp.max(work, axis=-1, keepdims=True)
        idx = jnp.min(jnp.where(work == mx, lane, float(LANES)), axis=-1, keepdims=True)
        oh = lane == idx
        vals.append(mx)
        idxs.append(idx)
        onehots.append(oh)
        work = jnp.where(oh, -jnp.inf, work)
    es = [jnp.exp(v - vals[0]) for v in vals]
    denom = es[0] + es[1] + es[2] + es[3]
    assigned = jnp.zeros((tm, LANES), F32)
    for oh in onehots:
        assigned = assigned + jnp.where(oh, 1.0, 0.0)
    r_i = lax.broadcasted_iota(jnp.int32, (tm, tm), 0)
    c_i = lax.broadcasted_iota(jnp.int32, (tm, tm), 1)
    ltri = jnp.where(c_i < r_i, 1.0, 0.0).astype(BF16)
    prior = _dot(ltri, assigned.astype(BF16)) + base_ref[...]
    meta = jnp.zeros((tm, LANES), F32)
    for jj in range(TOPK_EXPERTS):
        rank = jnp.sum(jnp.where(onehots[jj], prior, 0.0), axis=-1, keepdims=True)
        meta = jnp.where(lane == jj, idxs[jj], meta)
        meta = jnp.where(lane == TOPK_EXPERTS + jj, es[jj] / denom, meta)
        meta = jnp.where(lane == 2 * TOPK_EXPERTS + jj, rank, meta)
    meta_ref[0] = meta
    base = base_ref[...] + jnp.sum(assigned, axis=0, keepdims=True)
    base_ref[...] = base
    cnt_ref[...] = base


def _outproj(x, y_rnn, y_attn, mod6, attn_out_g, w_out, norm2_g, w_router, b_router, tm):
    B, S, D = x.shape
    wor = w_out[:D_RNN].astype(BF16)
    woa = w_out[D_RNN:].astype(BF16)
    wr = jnp.zeros((D, LANES), F32).at[:, :N_EXPERTS].set(w_router)
    wr_hi, wr_lo = _split(wr)
    br = jnp.full((1, LANES), NEG_BIG, F32).at[0, :N_EXPERTS].set(b_router)
    const = lambda shape: pl.BlockSpec(shape, lambda b, s: (0,) * len(shape))
    row = lambda n: pl.BlockSpec((1, tm, n), lambda b, s: (b, s, 0))
    return pl.pallas_call(
        functools.partial(_outproj_kernel, tm=tm),
        grid=(B, S // tm),
        in_specs=[row(D), row(D_RNN), row(D_ATTN), pl.BlockSpec((1, 6, D), lambda b, s: (b, 0, 0)),
                  const((1, D_ATTN)), const((D_RNN, D)), const((D_ATTN, D)), const((1, D)),
                  const((D, LANES)), const((D, LANES)), const((1, LANES))],
        out_specs=[row(D), row(D // 2), row(LANES), const((1, LANES))],
        out_shape=[jax.ShapeDtypeStruct((B, S, D), F32), jax.ShapeDtypeStruct((B, S, D // 2), jnp.int32),
                   jax.ShapeDtypeStruct((B, S, LANES), F32), jax.ShapeDtypeStruct((1, LANES), F32)],
        scratch_shapes=[pltpu.VMEM((1, LANES), F32)],
        compiler_params=_cparams("arbitrary", "arbitrary"),
        name="outproj",
    )(x, y_rnn, y_attn, mod6, attn_out_g.reshape(1, D_ATTN), wor, woa, norm2_g.reshape(1, D),
      wr_hi, wr_lo, br)


GLU_GROUP = 2 * LANES


def _degroup_kernel(w_ref, p_ref, o_ref):
    p = p_ref[...]
    for g in range(2 * D_FF // GLU_GROUP):
        cols = slice(g * GLU_GROUP, (g + 1) * GLU_GROUP)
        o_ref[0, :, cols] = _dot(w_ref[0, :, cols].astype(BF16), p).astype(BF16)


def _degroup_w1(w1):
    E, D, F2 = w1.shape
    src = jnp.arange(GLU_GROUP)
    dst = jnp.where(src % 2 == 0, src // 2, LANES + src // 2)
    perm = (dst[:, None] == jnp.arange(GLU_GROUP)[None, :]).astype(BF16)
    tr = 512
    return pl.pallas_call(
        _degroup_kernel,
        grid=(E, D // tr),
        in_specs=[pl.BlockSpec((1, tr, F2), lambda e, r: (e, r, 0)),
                  pl.BlockSpec((GLU_GROUP, GLU_GROUP), lambda e, r: (0, 0))],
        out_specs=pl.BlockSpec((1, tr, F2), lambda e, r: (e, r, 0)),
        out_shape=jax.ShapeDtypeStruct((E, D, F2), BF16),
        compiler_params=_cparams("arbitrary", "arbitrary"),
        name="degroup_w1",
    )(w1, perm)


def _expert_kernel(be_ref, nb_ref, x_ref, w1_ref, b1_ref, w2_ref, b2_ref, y_ref):
    del be_ref

    @pl.when(pl.program_id(0) < nb_ref[0])
    def _():
        x = _unpack_bf16_pairs(x_ref[...]).astype(BF16)
        u = _dot(x, w1_ref[0]) + b1_ref[0]
        acts = []
        for g in range(2 * D_FF // GLU_GROUP):
            ug = jnp.minimum(u[:, g * GLU_GROUP:g * GLU_GROUP + LANES], SWIGLU_LIMIT)
            ul = jnp.clip(u[:, g * GLU_GROUP + LANES:(g + 1) * GLU_GROUP], -SWIGLU_LIMIT, SWIGLU_LIMIT)
            acts.append((ug * jax.nn.sigmoid(SWIGLU_ALPHA * ug) * (ul + 1.0)).astype(BF16))
        act = jnp.concatenate(acts, axis=1)
        y_ref[...] = _pack_bf16_pairs(_dot(act, w2_ref[0]) + b2_ref[0])

    @pl.when(pl.program_id(0) >= nb_ref[0])
    def _():
        y_ref[...] = jnp.zeros(y_ref.shape, y_ref.dtype)


def _experts(x_sorted, block_e, n_used, w1, b1, w2, b2, mb):
    n_rows, half = x_sorted.shape
    D = 2 * half
    E = w1.shape[0]
    w1p = _degroup_w1(w1)
    ng = 2 * D_FF // GLU_GROUP
    b1p = jnp.swapaxes(b1.reshape(E, ng, LANES, 2), 2, 3).reshape(E, 1, 2 * D_FF)
    wspec = lambda k, n: pl.BlockSpec((1, k, n), lambda i, be, nb: (be[i], 0, 0))
    grid_spec = pltpu.PrefetchScalarGridSpec(
        num_scalar_prefetch=2,
        grid=(n_rows // mb,),
        in_specs=[pl.BlockSpec((mb, half), lambda i, be, nb: (jnp.minimum(i, nb[0] - 1), 0)),
                  wspec(D, 2 * D_FF), wspec(1, 2 * D_FF), wspec(D_FF, D), wspec(1, D)],
        out_specs=pl.BlockSpec((mb, half), lambda i, be, nb: (i, 0)),
    )
    return pl.pallas_call(
        _expert_kernel,
        grid_spec=grid_spec,
        out_shape=jax.ShapeDtypeStruct((n_rows, half), jnp.int32),
        compiler_params=_cparams("arbitrary"),
        name="experts",
    )(block_e, n_used, x_sorted, w1p, b1p, w2.astype(BF16), b2.reshape(E, 1, D))


def _combine_kernel(x1_ref, yg_ref, meta_ref, mod_ref, o_ref):
    meta = meta_ref[0]
    ff = meta[:, TOPK_EXPERTS:TOPK_EXPERTS + 1] * _unpack_bf16_pairs(yg_ref[0])
    for jj in range(1, TOPK_EXPERTS):
        ff = ff + meta[:, TOPK_EXPERTS + jj:TOPK_EXPERTS + jj + 1] * _unpack_bf16_pairs(yg_ref[jj])
    o_ref[0] = x1_ref[0] + mod_ref[0][5:6] * ff


def _combine(x1, yg, meta, mod6, tm):
    B, S, D = x1.shape
    nt = S // tm
    return pl.pallas_call(
        _combine_kernel,
        grid=(B, nt),
        in_specs=[pl.BlockSpec((1, tm, D), lambda b, s: (b, s, 0)),
                  pl.BlockSpec((TOPK_EXPERTS, tm, D // 2), lambda b, s: (0, b * nt + s, 0)),
                  pl.BlockSpec((1, tm, LANES), lambda b, s: (b, s, 0)),
                  pl.BlockSpec((1, 6, D), lambda b, s: (b, 0, 0))],
        out_specs=pl.BlockSpec((1, tm, D), lambda b, s: (b, s, 0)),
        out_shape=jax.ShapeDtypeStruct((B, S, D), F32),
        compiler_params=_cparams("arbitrary", "arbitrary"),
        name="combine",
    )(x1, yg, meta, mod6)


def _tile(n, pref):
    t = pref
    while n % t:
        t //= 2
    return t


def _layer(x, c, w_ada, b_ada, norm1_g, w_in, conv_w, conv_b, w_rg_a, b_rg_a, w_rg_x, b_rg_x,
           lru_lambda, q_norm_g, k_norm_g, kidx_norm_g, rg_out_g, attn_out_g, w_out, norm2_g,
           w_router, b_router, w1, b1, w2, b2):
    B, S, D = x.shape
    T = B * S
    tm = _tile(S, 512)
    mod6 = _ada(c, w_ada, b_ada).reshape(B, 6, D)

    xr, xg, q, k, v, qi, kiw = _inproj(x, mod6, norm1_g, w_in, q_norm_g, k_norm_g, kidx_norm_g, tm)
    y_rnn = _rglru(xr, xg, conv_w, conv_b, w_rg_a, b_rg_a, w_rg_x, b_rg_x, lru_lambda, rg_out_g,
                   _tile(S, 256))

    ki_ext = kiw[..., :IDX_DIM].astype(BF16)
    qi_ext = jnp.transpose(qi.reshape(B, S, IDX_HEADS, IDX_DIM), (0, 2, 3, 1))
    q_t = jnp.swapaxes(q, 1, 2)
    w_t = jnp.swapaxes(kiw[..., IDX_DIM:IDX_DIM + IDX_HEADS], 1, 2)
    v_h = v.reshape(B, S, N_HEADS, HEAD_DIM)
    v_ext = jnp.concatenate([v_h, jnp.ones((B, S, N_HEADS, 1), BF16),
                             jnp.zeros((B, S, N_HEADS, V_ROWS - HEAD_DIM - 1), BF16)], axis=-1)
    vt = jnp.swapaxes(v_ext.reshape(B, S, N_HEADS * V_ROWS), 1, 2)
    qb = _tile(S, 256)
    y_attn_t = _dsa(ki_ext, k, vt, qi_ext, q_t, w_t, qb, _tile(S, 512))
    y_attn = jnp.swapaxes(y_attn_t, 1, 2)

    x1, h2, meta, cnt = _outproj(x, y_rnn, y_attn, mod6, attn_out_g, w_out, norm2_g,
                                 w_router, b_router, tm)

    mb = 512
    top_e = meta[..., 0:TOPK_EXPERTS].astype(jnp.int32).reshape(T, TOPK_EXPERTS)
    rank = meta[..., 2 * TOPK_EXPERTS:3 * TOPK_EXPERTS].astype(jnp.int32).reshape(T, TOPK_EXPERTS)
    counts = cnt[0, :N_EXPERTS].astype(jnp.int32)
    padded = (counts + mb - 1) // mb * mb
    pad_end = jnp.cumsum(padded)
    pad_start = pad_end - padded
    dest = pad_start[top_e] + rank
    n_rows = T * TOPK_EXPERTS + N_EXPERTS * mb
    n_blocks = n_rows // mb
    block_start = jnp.arange(n_blocks, dtype=jnp.int32) * mb
    block_e = jnp.minimum(jnp.sum((pad_end[None, :] <= block_start[:, None]).astype(jnp.int32), axis=1),
                          N_EXPERTS - 1)
    tok = jnp.broadcast_to(jnp.arange(T, dtype=jnp.int32)[:, None], (T, TOPK_EXPERTS))
    _, sorted_tok = lax.sort_key_val(dest.reshape(-1), tok.reshape(-1))
    group_start = jnp.cumsum(counts) - counts
    pad_before = jnp.repeat((pad_start - group_start)[block_e], mb)
    pos = jnp.clip(jnp.arange(n_rows, dtype=jnp.int32) - pad_before, 0, T * TOPK_EXPERTS - 1)
    row_tok = sorted_tok[pos]
    x_sorted = h2.reshape(T, D // 2)[row_tok]
    n_used = (pad_end[N_EXPERTS - 1:] // mb).astype(jnp.int32)
    y_sorted = _experts(x_sorted, block_e, n_used, w1, b1, w2, b2, mb)
    yg = y_sorted[dest.T]
    return _combine(x1, yg, meta, mod6, tm)


def kernel(x, c, w_ada, b_ada, norm1_g, w_in, conv_w, conv_b, w_rg_a, b_rg_a, w_rg_x, b_rg_x,
           lru_lambda, q_norm_g, k_norm_g, kidx_norm_g, rg_out_g, attn_out_g, w_out, norm2_g,
           w_router, b_router, w1, b1, w2, b2):
    depth = w_ada.shape[0]
    for l in range(depth):
        x = _layer(x, c, w_ada[l], b_ada[l], norm1_g[l], w_in[l], conv_w[l], conv_b[l],
                   w_rg_a[l], b_rg_a[l], w_rg_x[l], b_rg_x[l], lru_lambda[l], q_norm_g[l],
                   k_norm_g[l], kidx_norm_g[l], rg_out_g[l], attn_out_g[l], w_out[l], norm2_g[l],
                   w_router[l], b_router[l], w1[l], b1[l], w2[l], b2[l])
    return x
```

```python
import functools

import jax
import jax.numpy as jnp
from jax import lax
from jax.experimental import pallas as pl
from jax.experimental.pallas import tpu as pltpu

D_MODEL = 1024
CHUNK = 64
CHUNK_SHIFT = 6
D_RNN = 512
RNN_BLOCKS = 8
RNN_BLOCK = 64
CONV_WIDTH = 4
RG_C = 8.0
N_HEADS = 8
HEAD_DIM = 64
D_ATTN = 512
IDX_HEADS = 8
IDX_DIM = 64
TOPK_KEYS_MAX = 256
N_EXPERTS = 32
TOPK_EXPERTS = 4
D_FF = 1024
SWIGLU_ALPHA = 1.702
SWIGLU_LIMIT = 7.0
EPS = 1e-6

LANES = 128
SUBLANES = 8
TINY_F32 = 1.1754943508222875e-38
REDUCE_CHAINS = 8
IDX_K = IDX_DIM
NEG_BIG = -(2.0 ** 100)
LOG2E = 1.4426950408889634
V_ROWS = 80
VMEM_LIMIT = 56 * 1024 * 1024

F32 = jnp.float32
BF16 = jnp.bfloat16


def _dot(a, b):
    return jnp.dot(a, b, preferred_element_type=F32)


def _dot_nt(a, b):
    return lax.dot_general(a, b, (((1,), (1,)), ((), ())), preferred_element_type=F32)


def _split(a):
    hi = a.astype(BF16)
    lo = (a - hi.astype(F32)).astype(BF16)
    return hi, lo


def _dot3(a_hi, a_lo, b_hi, b_lo):
    return _dot(a_hi, b_hi) + (_dot(a_hi, b_lo) + _dot(a_lo, b_hi))


def _pack_bf16_pairs(a):
    n = a.shape[1] // 2
    hi = lax.bitcast_convert_type(a[:, :n].astype(BF16).astype(F32), jnp.int32)
    lo = lax.bitcast_convert_type(a[:, n:].astype(BF16).astype(F32), jnp.int32)
    return jnp.bitwise_or(hi, lax.shift_right_logical(lo, 16))


def _unpack_bf16_pairs(u):
    hi = lax.bitcast_convert_type(jnp.bitwise_and(u, -65536), F32)
    lo = lax.bitcast_convert_type(lax.shift_left(u, 16), F32)
    return jnp.concatenate([hi, lo], axis=1)


def _rms(x, g):
    ms = jnp.mean(x * x, axis=-1, keepdims=True)
    return x * lax.rsqrt(ms + EPS) * g


def _cparams(*sem):
    return pltpu.CompilerParams(dimension_semantics=sem, vmem_limit_bytes=VMEM_LIMIT)


def _ada_kernel(c_ref, w_ref, b_ref, o_ref):
    c = c_ref[...]
    s = c * jax.nn.sigmoid(c)
    s_hi, s_lo = _split(s)
    w_hi, w_lo = _split(w_ref[...])
    o_ref[...] = _dot3(s_hi, s_lo, w_hi, w_lo) + b_ref[...]


def _ada(c, w_ada, b_ada):
    B, D = c.shape
    N = w_ada.shape[1]
    rows = 16
    c_pad = jnp.zeros((rows, D), F32).at[:B].set(c)
    tn = 1024
    out = pl.pallas_call(
        _ada_kernel,
        grid=(N // tn,),
        in_specs=[pl.BlockSpec((rows, D), lambda j: (0, 0)),
                  pl.BlockSpec((D, tn), lambda j: (0, j)),
                  pl.BlockSpec((1, tn), lambda j: (0, j))],
        out_specs=pl.BlockSpec((rows, tn), lambda j: (0, j)),
        out_shape=jax.ShapeDtypeStruct((rows, N), F32),
        compiler_params=_cparams("arbitrary"),
        name="ada",
    )(c_pad, w_ada, b_ada.reshape(1, N))
    return out[:B]


def _inproj_kernel(x_ref, mod_ref, g1_ref, wmain_ref, widx_ref, bd_ref,
                   qg_ref, kg_ref, kig_ref,
                   xr_ref, xg_ref, q_ref, k_ref, v_ref, qi_ref, kiw_ref):
    x = x_ref[0]
    mod = mod_ref[0]
    h = _rms(x, g1_ref[...]) * (1.0 + mod[1:2]) + mod[0:1]
    h_hi = h.astype(BF16)
    bd = bd_ref[...]

    def head_norm(z, g):
        sq_hi, sq_lo = _split(z * z)
        ms = _dot(sq_hi, bd) + _dot(sq_lo, bd)
        return z * lax.rsqrt(ms + EPS) * g

    xr_ref[0] = _dot(h_hi, wmain_ref[:, 0:512])
    xg_ref[0] = _dot(h_hi, wmain_ref[:, 512:1024])
    zq = _dot(h_hi, wmain_ref[:, 1024:1536])
    q_ref[0] = (head_norm(zq, qg_ref[...]) * (HEAD_DIM ** -0.5 * LOG2E)).astype(BF16)
    zk = _dot(h_hi, wmain_ref[:, 1536:2048])
    k_ref[0] = head_norm(zk, kg_ref[...]).astype(BF16)
    v_ref[0] = _dot(h_hi, wmain_ref[:, 2048:2560]).astype(BF16)

    zi = _dot(h_hi, widx_ref[...])
    qi_ref[0] = (zi[:, 0:512] * (IDX_DIM ** -0.5)).astype(BF16)
    kw = zi[:, 512:640]
    lane = lax.broadcasted_iota(jnp.int32, kw.shape, 1)
    is_ki = lane < IDX_DIM
    ms = jnp.sum(jnp.where(is_ki, kw * kw, 0.0), axis=-1, keepdims=True) * (1.0 / IDX_DIM)
    kiw_ref[0] = jnp.where(is_ki, kw * lax.rsqrt(ms + EPS) * kig_ref[...], kw * (IDX_HEADS ** -0.5))


def _inproj(x, mod6, norm1_g, w_in, q_norm_g, k_norm_g, kidx_norm_g, tm):
    B, S, D = x.shape
    wmain = w_in[:, :2560].astype(BF16)
    widx = jnp.concatenate([w_in[:, 2560:3144], jnp.zeros((D, 640 - 584), F32)], axis=1).astype(BF16)
    head = jnp.arange(D_ATTN) // HEAD_DIM
    bd = jnp.where(head[:, None] == head[None, :], 1.0 / HEAD_DIM, 0.0).astype(BF16)
    qg = jnp.tile(q_norm_g, N_HEADS).reshape(1, D_ATTN)
    kg = jnp.tile(k_norm_g, N_HEADS).reshape(1, D_ATTN)
    kig = jnp.concatenate([kidx_norm_g, jnp.zeros((LANES - IDX_DIM,), F32)]).reshape(1, LANES)
    const = lambda shape: pl.BlockSpec(shape, lambda b, s: (0,) * len(shape))
    row = lambda n: pl.BlockSpec((1, tm, n), lambda b, s: (b, s, 0))
    return pl.pallas_call(
        _inproj_kernel,
        grid=(B, S // tm),
        in_specs=[row(D), pl.BlockSpec((1, 6, D), lambda b, s: (b, 0, 0)), const((1, D)),
                  const((D, 2560)), const((D, 640)), const((D_ATTN, D_ATTN)),
                  const((1, D_ATTN)), const((1, D_ATTN)), const((1, LANES))],
        out_specs=[row(512), row(512), row(512), row(512), row(512), row(512), row(LANES)],
        out_shape=[jax.ShapeDtypeStruct((B, S, 512), F32), jax.ShapeDtypeStruct((B, S, 512), F32),
                   jax.ShapeDtypeStruct((B, S, 512), BF16), jax.ShapeDtypeStruct((B, S, 512), BF16),
                   jax.ShapeDtypeStruct((B, S, 512), BF16), jax.ShapeDtypeStruct((B, S, 512), BF16),
                   jax.ShapeDtypeStruct((B, S, LANES), F32)],
        compiler_params=_cparams("arbitrary", "arbitrary"),
        name="inproj",
    )(x, mod6, norm1_g.reshape(1, D), wmain, widx, bd, qg, kg, kig)


def _rglru_kernel(xr_ref, xg_ref, cw_ref, cb_ref, wa_hi_ref, wa_lo_ref, wx_hi_ref, wx_lo_ref,
                  ba_ref, bx_ref, lam_ref, g_ref, y_ref, ext_ref, hlast_ref, *, ts):
    @pl.when(pl.program_id(1) == 0)
    def _():
        ext_ref[0:8, :] = jnp.zeros((8, D_RNN), F32)
        hlast_ref[...] = jnp.zeros((1, D_RNN), F32)

    xr = xr_ref[0]
    ext_ref[8:8 + ts, :] = xr
    cw = cw_ref[...]
    xc = cb_ref[...] + cw[0:1] * ext_ref[5:5 + ts, :]
    for j in range(1, CONV_WIDTH):
        xc = xc + cw[j:j + 1] * ext_ref[5 + j:5 + j + ts, :]
    ext_ref[0:8, :] = xr[ts - 8:ts]

    xc_hi, xc_lo = _split(xc)
    r = jax.nn.sigmoid(_dot3(xc_hi, xc_lo, wa_hi_ref[...], wa_lo_ref[...]) + ba_ref[...])
    i = jax.nn.sigmoid(_dot3(xc_hi, xc_lo, wx_hi_ref[...], wx_lo_ref[...]) + bx_ref[...])
    lam = lam_ref[...]
    sp = jnp.maximum(-lam, 0.0) + jnp.log1p(jnp.exp(-jnp.abs(lam)))
    log_a = (-RG_C) * r * sp
    a = jnp.exp(log_a)
    mult = jnp.sqrt(-jnp.tanh(log_a) * (a * a + 1.0))
    u = mult * i * xc

    row = lax.broadcasted_iota(jnp.int32, (ts, D_RNN), 0)
    acc_a, acc_h = a, u
    d = 1
    while d < ts:
        sh_a = pltpu.roll(acc_a, d, 0)
        sh_h = pltpu.roll(acc_h, d, 0)
        ok = row >= d
        acc_h = jnp.where(ok, acc_a * sh_h + acc_h, acc_h)
        acc_a = jnp.where(ok, acc_a * sh_a, acc_a)
        d *= 2
    hseq = acc_h + acc_a * hlast_ref[...]
    hlast_ref[...] = hseq[ts - 1:ts]

    xg = xg_ref[0]
    gelu = 0.5 * xg * (1.0 + jnp.tanh(0.7978845608028654 * (xg + 0.044715 * (xg * xg * xg))))
    y_ref[0] = _rms(gelu * hseq, g_ref[...]).astype(BF16)


def _block_diag(w):
    H, n, _ = w.shape
    eye = jnp.eye(H, dtype=w.dtype)
    return (eye[:, None, :, None] * w[:, :, None, :]).reshape(H * n, H * n)


def _rglru(xr, xg, conv_w, conv_b, w_a, b_a, w_x, b_x, lam, g, ts):
    B, S, _ = xr.shape
    wa_hi, wa_lo = _split(_block_diag(w_a))
    wx_hi, wx_lo = _split(_block_diag(w_x))
    const = lambda shape: pl.BlockSpec(shape, lambda b, s: (0,) * len(shape))
    row = pl.BlockSpec((1, ts, D_RNN), lambda b, s: (b, s, 0))
    vec = lambda a: a.reshape(1, D_RNN)
    return pl.pallas_call(
        functools.partial(_rglru_kernel, ts=ts),
        grid=(B, S // ts),
        in_specs=[row, row, const((CONV_WIDTH, D_RNN)), const((1, D_RNN)),
                  const((D_RNN, D_RNN)), const((D_RNN, D_RNN)), const((D_RNN, D_RNN)), const((D_RNN, D_RNN)),
                  const((1, D_RNN)), const((1, D_RNN)), const((1, D_RNN)), const((1, D_RNN))],
        out_specs=row,
        out_shape=jax.ShapeDtypeStruct((B, S, D_RNN), BF16),
        scratch_shapes=[pltpu.VMEM((ts + 8, D_RNN), F32), pltpu.VMEM((1, D_RNN), F32)],
        compiler_params=_cparams("arbitrary", "arbitrary"),
        name="rglru",
    )(xr, xg, conv_w, vec(conv_b), wa_hi, wa_lo, wx_hi, wx_lo, vec(b_a), vec(b_x), vec(lam), vec(g))


def _dsa_kernel(ki_ref, k_ref, vt_ref, qi_ref, q_ref, w_ref, y_ref,
                sc_ref, qh_ref, acc_ref, lga_ref, *, qb, tk, n_sel, seq):
    j = pl.program_id(1)
    n_kt = ((j + 1) * qb + tk - 1) // tk
    q_idx = j * qb + lax.broadcasted_iota(jnp.int32, (1, qb), 1)
    q_chunk = jnp.right_shift(q_idx, CHUNK_SHIFT)

    def key_index(kt):
        return kt * tk + lax.broadcasted_iota(jnp.int32, (tk, 1), 0)

    w = w_ref[0]

    def fold8(a):
        return a.reshape(tk // SUBLANES, SUBLANES, qb)

    def count8(off, pred):
        rows = tk // REDUCE_CHAINS
        parts = []
        for i in range(REDUCE_CHAINS):
            blk = sc_ref[pl.ds(off + i * rows, rows), :]
            ind = jnp.where(pred(blk, i * rows), 1.0, 0.0)
            parts.append(jnp.sum(ind.reshape(rows // SUBLANES, SUBLANES, qb), axis=0))
        while len(parts) > 1:
            parts = [parts[i] + parts[i + 1] for i in range(0, len(parts), 2)]
        return parts[0]

    def score_tile(kt, carry):
        smin, smax = carry
        off = pl.multiple_of(kt * tk, tk)
        ki = ki_ref[0, pl.ds(off, tk), :]
        sc = jnp.zeros((tk, qb), F32)
        for h in range(IDX_HEADS):
            s = _dot(ki, qi_ref[0, h])
            sc = sc + w[h:h + 1] * jnp.maximum(s, 0.0)
        adm = jnp.right_shift(key_index(kt), CHUNK_SHIFT) <= q_chunk
        sc_ref[pl.ds(off, tk), :] = jnp.where(adm, sc, -jnp.inf)
        smin = jnp.minimum(smin, jnp.min(fold8(jnp.where(adm, sc, jnp.inf)), axis=0))
        smax = jnp.maximum(smax, jnp.max(fold8(jnp.where(adm, sc, -jnp.inf)), axis=0))
        return smin, smax

    smin8, smax8 = lax.fori_loop(0, n_kt, score_tile,
                                 (jnp.full((SUBLANES, qb), jnp.inf, F32),
                                  jnp.full((SUBLANES, qb), -jnp.inf, F32)))
    smin = jnp.min(smin8, axis=0, keepdims=True)
    smax = jnp.max(smax8, axis=0, keepdims=True)

    def count_ge(t):
        def body(kt, c8):
            off = pl.multiple_of(kt * tk, tk)
            return c8 + count8(off, lambda blk, r0: blk >= t)
        c8 = lax.fori_loop(0, n_kt, body, jnp.zeros((SUBLANES, qb), F32))
        return jnp.sum(c8, axis=0, keepdims=True)

    n_adm = (q_chunk + 1) * CHUNK
    take_all = n_adm <= n_sel
    nf = float(n_sel)

    def probe_step(st, t):
        lo, hi, cnt_hi, thr, exact, done_f = st
        active = done_f == 0.0
        c = count_ge(t)
        inside = jnp.logical_and(t > lo, t < hi)
        stuck = jnp.logical_and(active, jnp.logical_not(inside))
        probed = jnp.logical_and(active, inside)
        hit = jnp.logical_and(probed, c == nf)
        move_lo = jnp.logical_and(probed, c > nf)
        move_hi = jnp.logical_and(probed, c < nf)
        finished = jnp.logical_or(hit, stuck)
        return (jnp.where(move_lo, t, lo), jnp.where(move_hi, t, hi), jnp.where(move_hi, c, cnt_hi),
                jnp.where(hit, t, jnp.where(stuck, lo, thr)), jnp.where(hit, 1.0, exact),
                jnp.where(finished, 1.0, done_f))

    def midpoint(lo, hi):
        return lo + (hi - lo) * 0.5

    zeros = jnp.zeros((1, qb), F32)
    take_all_f = jnp.where(take_all, 1.0, 0.0)
    c_top = count_ge(smax)
    top_hit = c_top == nf
    top_ge = c_top >= nf
    live = jnp.logical_not(take_all)
    st = (jnp.where(jnp.logical_and(live, top_ge), smax, smin), smax,
          jnp.where(jnp.logical_and(live, jnp.logical_not(top_ge)), c_top, zeros),
          jnp.where(jnp.logical_and(live, top_hit), smax, jnp.full((1, qb), -jnp.inf, F32)),
          jnp.where(jnp.logical_and(live, top_hit), 1.0, take_all_f),
          jnp.where(jnp.logical_and(live, top_hit), 1.0, take_all_f))
    for special in (0.0, TINY_F32):
        lo, hi = st[0], st[1]
        sp = jnp.full((1, qb), special, F32)
        st = probe_step(st, jnp.where(jnp.logical_and(sp > lo, sp < hi), sp, midpoint(lo, hi)))

    def bis_cond(st):
        return jnp.min(st[5]) < 1.0

    def bis_body(st):
        return probe_step(st, midpoint(st[0], st[1]))

    lo, hi, cnt_hi, thr, exact, _ = lax.while_loop(bis_cond, bis_body, st)
    need = nf - cnt_hi

    def count_tie(jcut):
        def body(kt, c8):
            off = pl.multiple_of(kt * tk, tk)

            def tied_before_cut(blk, r0):
                key = off + r0 + lax.broadcasted_iota(jnp.int32, (blk.shape[0], 1), 0)
                return jnp.logical_and(blk == thr, key <= jcut)
            return c8 + count8(off, tied_before_cut)
        c8 = lax.fori_loop(0, n_kt, body, jnp.zeros((SUBLANES, qb), F32))
        return jnp.sum(c8, axis=0, keepdims=True)

    def tie_cond(st):
        jlo, jhi = st
        return jnp.max(jnp.where(jnp.logical_and(exact == 0.0, jlo < jhi), 1.0, 0.0)) > 0.0

    def tie_body(st):
        jlo, jhi = st
        jmid = jnp.right_shift(jlo + jhi, 1)
        ok = count_tie(jmid) >= need
        return jnp.where(ok, jlo, jmid + 1), jnp.where(ok, jmid, jhi)

    jlo, _ = lax.while_loop(tie_cond, tie_body,
                            (jnp.zeros((1, qb), jnp.int32), jnp.full((1, qb), seq - 1, jnp.int32)))
    jcut = jnp.where(take_all, -1, jnp.where(exact > 0.0, seq, jlo))

    acc_ref[...] = jnp.zeros((N_HEADS * V_ROWS, qb), F32)
    zero_half = jnp.zeros((HEAD_DIM, qb), BF16)
    for p in range(N_HEADS // 2):
        qh_ref[2 * p] = jnp.concatenate([q_ref[0, p * LANES:p * LANES + HEAD_DIM, :], zero_half], axis=0)
        qh_ref[2 * p + 1] = jnp.concatenate([zero_half, q_ref[0, p * LANES + HEAD_DIM:(p + 1) * LANES, :]],
                                            axis=0)

    def masked_logits(kt, dst_ref):
        off = pl.multiple_of(kt * tk, tk)
        blk = sc_ref[pl.ds(off, tk), :]
        sel = jnp.logical_or(blk > thr, jnp.logical_and(blk == thr, key_index(kt) <= jcut))
        bias = jnp.where(sel, 0.0, NEG_BIG).astype(BF16)
        for h in range(N_HEADS):
            p = h // 2
            kp = k_ref[0, pl.ds(off, tk), p * LANES:(p + 1) * LANES]
            dst_ref[h] = _dot(kp, qh_ref[h]).astype(BF16) + bias

    def softmax_pv(kt, src_ref, ms):
        off = pl.multiple_of(kt * tk, tk)
        m_new = []
        for h in range(N_HEADS):
            logits = src_ref[h]
            m_h = jnp.maximum(ms[h], jnp.max(logits, axis=0, keepdims=True).astype(F32))
            alpha = jnp.exp2(ms[h] - m_h)
            probs = jnp.exp2(logits - m_h.astype(BF16))
            vt = vt_ref[0, h * V_ROWS:(h + 1) * V_ROWS, pl.ds(off, tk)]
            rows = slice(h * V_ROWS, (h + 1) * V_ROWS)
            acc_ref[rows, :] = alpha * acc_ref[rows, :] + _dot(vt, probs)
            m_new.append(m_h)
        return tuple(m_new)

    def attn_tile(kt, ms):
        masked_logits(kt, lga_ref)
        return softmax_pv(kt, lga_ref, ms)

    m0 = tuple(jnp.full((1, qb), NEG_BIG, F32) for _ in range(N_HEADS))
    lax.fori_loop(0, n_kt, attn_tile, m0)
    for h in range(N_HEADS):
        num = acc_ref[h * V_ROWS:h * V_ROWS + HEAD_DIM, :]
        den = acc_ref[h * V_ROWS + HEAD_DIM:h * V_ROWS + HEAD_DIM + 1, :]
        y_ref[0, h * HEAD_DIM:(h + 1) * HEAD_DIM, :] = num / den


def _dsa(ki_ext, k, vt, qi_ext, q, w_t, qb, tk):
    B, S, _ = k.shape
    n_sel = min(TOPK_KEYS_MAX, S // 4)
    resident = lambda shape: pl.BlockSpec(shape, lambda b, j: (b, 0, 0), pipeline_mode=pl.Buffered(1))
    return pl.pallas_call(
        functools.partial(_dsa_kernel, qb=qb, tk=tk, n_sel=n_sel, seq=S),
        grid=(B, S // qb),
        in_specs=[resident((1, S, IDX_K)), resident((1, S, D_ATTN)), resident((1, N_HEADS * V_ROWS, S)),
                  pl.BlockSpec((1, IDX_HEADS, IDX_K, qb), lambda b, j: (b, 0, 0, j)),
                  pl.BlockSpec((1, D_ATTN, qb), lambda b, j: (b, 0, j)),
                  pl.BlockSpec((1, IDX_HEADS, qb), lambda b, j: (b, 0, j))],
        out_specs=pl.BlockSpec((1, D_ATTN, qb), lambda b, j: (b, 0, j)),
        out_shape=jax.ShapeDtypeStruct((B, D_ATTN, S), F32),
        scratch_shapes=[pltpu.VMEM((S, qb), F32), pltpu.VMEM((N_HEADS, LANES, qb), BF16),
                        pltpu.VMEM((N_HEADS * V_ROWS, qb), F32),
                        pltpu.VMEM((N_HEADS, tk, qb), BF16)],
        compiler_params=_cparams("arbitrary", "arbitrary"),
        name="dsa",
    )(ki_ext, k, vt, qi_ext, q, w_t)


def _outproj_kernel(x_ref, yr_ref, ya_ref, mod_ref, ag_ref, wor_ref, woa_ref, n2g_ref,
                    wr_hi_ref, wr_lo_ref, br_ref,
                    x1_ref, h2_ref, meta_ref, cnt_ref, base_ref, *, tm):
    @pl.when(jnp.logical_and(pl.program_id(0) == 0, pl.program_id(1) == 0))
    def _():
        base_ref[...] = jnp.zeros((1, LANES), F32)

    mod = mod_ref[0]
    ya = _rms(ya_ref[0], ag_ref[...]).astype(BF16)
    mix = _dot(yr_ref[0], wor_ref[...]) + _dot(ya, woa_ref[...])
    x1 = x_ref[0] + mod[2:3] * mix
    x1_ref[0] = x1
    h2 = _rms(x1, n2g_ref[...]) * (1.0 + mod[4:5]) + mod[3:4]
    h2_ref[0] = _pack_bf16_pairs(h2)
    h2_hi, h2_lo = _split(h2)
    logits = _dot3(h2_hi, h2_lo, wr_hi_ref[...], wr_lo_ref[...]) + br_ref[...]

    lane = lax.broadcasted_iota(jnp.int32, (tm, LANES), 1).astype(F32)
    work = logits
    vals, onehots, idxs = [], [], []
    for _ in range(TOPK_EXPERTS):
        mx = jnp.max(work, axis=-1, keepdims=True)
        idx = jnp.min(jnp.where(work == mx, lane, float(LANES)), axis=-1, keepdims=True)
        oh = lane == idx
        vals.append(mx)
        idxs.append(idx)
        onehots.append(oh)
        work = jnp.where(oh, -jnp.inf, work)
    es = [jnp.exp(v - vals[0]) for v in vals]
    denom = es[0] + es[1] + es[2] + es[3]
    assigned = jnp.zeros((tm, LANES), F32)
    for oh in onehots:
        assigned = assigned + jnp.where(oh, 1.0, 0.0)
    r_i = lax.broadcasted_iota(jnp.int32, (tm, tm), 0)
    c_i = lax.broadcasted_iota(jnp.int32, (tm, tm), 1)
    ltri = jnp.where(c_i < r_i, 1.0, 0.0).astype(BF16)
    prior = _dot(ltri, assigned.astype(BF16)) + base_ref[...]
    meta = jnp.zeros((tm, LANES), F32)
    for jj in range(TOPK_EXPERTS):
        rank = jnp.sum(jnp.where(onehots[jj], prior, 0.0), axis=-1, keepdims=True)
        meta = jnp.where(lane == jj, idxs[jj], meta)
        meta = jnp.where(lane == TOPK_EXPERTS + jj, es[jj] / denom, meta)
        meta = jnp.where(lane == 2 * TOPK_EXPERTS + jj, rank, meta)
    meta_ref[0] = meta
    base = base_ref[...] + jnp.sum(assigned, axis=0, keepdims=True)
    base_ref[...] = base
    cnt_ref[...] = base


def _outproj(x, y_rnn, y_attn, mod6, attn_out_g, w_out, norm2_g, w_router, b_router, tm):
    B, S, D = x.shape
    wor = w_out[:D_RNN].astype(BF16)
    woa = w_out[D_RNN:].astype(BF16)
    wr = jnp.zeros((D, LANES), F32).at[:, :N_EXPERTS].set(w_router)
    wr_hi, wr_lo = _split(wr)
    br = jnp.full((1, LANES), NEG_BIG, F32).at[0, :N_EXPERTS].set(b_router)
    const = lambda shape: pl.BlockSpec(shape, lambda b, s: (0,) * len(shape))
    row = lambda n: pl.BlockSpec((1, tm, n), lambda b, s: (b, s, 0))
    return pl.pallas_call(
        functools.partial(_outproj_kernel, tm=tm),
        grid=(B, S // tm),
        in_specs=[row(D), row(D_RNN), row(D_ATTN), pl.BlockSpec((1, 6, D), lambda b, s: (b, 0, 0)),
                  const((1, D_ATTN)), const((D_RNN, D)), const((D_ATTN, D)), const((1, D)),
                  const((D, LANES)), const((D, LANES)), const((1, LANES))],
        out_specs=[row(D), row(D // 2), row(LANES), const((1, LANES))],
        out_shape=[jax.ShapeDtypeStruct((B, S, D), F32), jax.ShapeDtypeStruct((B, S, D // 2), jnp.int32),
                   jax.ShapeDtypeStruct((B, S, LANES), F32), jax.ShapeDtypeStruct((1, LANES), F32)],
        scratch_shapes=[pltpu.VMEM((1, LANES), F32)],
        compiler_params=_cparams("arbitrary", "arbitrary"),
        name="outproj",
    )(x, y_rnn, y_attn, mod6, attn_out_g.reshape(1, D_ATTN), wor, woa, norm2_g.reshape(1, D),
      wr_hi, wr_lo, br)


GLU_GROUP = 2 * LANES


def _degroup_kernel(w_ref, p_ref, o_ref):
    p = p_ref[...]
    for g in range(2 * D_FF // GLU_GROUP):
        cols = slice(g * GLU_GROUP, (g + 1) * GLU_GROUP)
        o_ref[0, :, cols] = _dot(w_ref[0, :, cols].astype(BF16), p).astype(BF16)


def _degroup_w1(w1):
    E, D, F2 = w1.shape
    src = jnp.arange(GLU_GROUP)
    dst = jnp.where(src % 2 == 0, src // 2, LANES + src // 2)
    perm = (dst[:, None] == jnp.arange(GLU_GROUP)[None, :]).astype(BF16)
    tr = 512
    return pl.pallas_call(
        _degroup_kernel,
        grid=(E, D // tr),
        in_specs=[pl.BlockSpec((1, tr, F2), lambda e, r: (e, r, 0)),
                  pl.BlockSpec((GLU_GROUP, GLU_GROUP), lambda e, r: (0, 0))],
        out_specs=pl.BlockSpec((1, tr, F2), lambda e, r: (e, r, 0)),
        out_shape=jax.ShapeDtypeStruct((E, D, F2), BF16),
        compiler_params=_cparams("arbitrary", "arbitrary"),
        name="degroup_w1",
    )(w1, perm)


def _expert_kernel(be_ref, nb_ref, x_ref, w1_ref, b1_ref, w2_ref, b2_ref, y_ref):
    del be_ref

    @pl.when(pl.program_id(0) < nb_ref[0])
    def _():
        x = _unpack_bf16_pairs(x_ref[...]).astype(BF16)
        u = _dot(x, w1_ref[0]) + b1_ref[0]
        acts = []
        for g in range(2 * D_FF // GLU_GROUP):
            ug = jnp.minimum(u[:, g * GLU_GROUP:g * GLU_GROUP + LANES], SWIGLU_LIMIT)
            ul = jnp.clip(u[:, g * GLU_GROUP + LANES:(g + 1) * GLU_GROUP], -SWIGLU_LIMIT, SWIGLU_LIMIT)
            acts.append((ug * jax.nn.sigmoid(SWIGLU_ALPHA * ug) * (ul + 1.0)).astype(BF16))
        act = jnp.concatenate(acts, axis=1)
        y_ref[...] = _pack_bf16_pairs(_dot(act, w2_ref[0]) + b2_ref[0])

    @pl.when(pl.program_id(0) >= nb_ref[0])
    def _():
        y_ref[...] = jnp.zeros(y_ref.shape, y_ref.dtype)


def _experts(x_sorted, block_e, n_used, w1, b1, w2, b2, mb):
    n_rows, half = x_sorted.shape
    D = 2 * half
    E = w1.shape[0]
    w1p = _degroup_w1(w1)
    ng = 2 * D_FF // GLU_GROUP
    b1p = jnp.swapaxes(b1.reshape(E, ng, LANES, 2), 2, 3).reshape(E, 1, 2 * D_FF)
    wspec = lambda k, n: pl.BlockSpec((1, k, n), lambda i, be, nb: (be[i], 0, 0))
    grid_spec = pltpu.PrefetchScalarGridSpec(
        num_scalar_prefetch=2,
        grid=(n_rows // mb,),
        in_specs=[pl.BlockSpec((mb, half), lambda i, be, nb: (jnp.minimum(i, nb[0] - 1), 0)),
                  wspec(D, 2 * D_FF), wspec(1, 2 * D_FF), wspec(D_FF, D), wspec(1, D)],
        out_specs=pl.BlockSpec((mb, half), lambda i, be, nb: (i, 0)),
    )
    return pl.pallas_call(
        _expert_kernel,
        grid_spec=grid_spec,
        out_shape=jax.ShapeDtypeStruct((n_rows, half), jnp.int32),
        compiler_params=_cparams("arbitrary"),
        name="experts",
    )(block_e, n_used, x_sorted, w1p, b1p, w2.astype(BF16), b2.reshape(E, 1, D))


def _combine_kernel(x1_ref, yg_ref, meta_ref, mod_ref, o_ref):
    meta = meta_ref[0]
    ff = meta[:, TOPK_EXPERTS:TOPK_EXPERTS + 1] * _unpack_bf16_pairs(yg_ref[0])
    for jj in range(1, TOPK_EXPERTS):
        ff = ff + meta[:, TOPK_EXPERTS + jj:TOPK_EXPERTS + jj + 1] * _unpack_bf16_pairs(yg_ref[jj])
    o_ref[0] = x1_ref[0] + mod_ref[0][5:6] * ff


def _combine(x1, yg, meta, mod6, tm):
    B, S, D = x1.shape
    nt = S // tm
    return pl.pallas_call(
        _combine_kernel,
        grid=(B, nt),
        in_specs=[pl.BlockSpec((1, tm, D), lambda b, s: (b, s, 0)),
                  pl.BlockSpec((TOPK_EXPERTS, tm, D // 2), lambda b, s: (0, b * nt + s, 0)),
                  pl.BlockSpec((1, tm, LANES), lambda b, s: (b, s, 0)),
                  pl.BlockSpec((1, 6, D), lambda b, s: (b, 0, 0))],
        out_specs=pl.BlockSpec((1, tm, D), lambda b, s: (b, s, 0)),
        out_shape=jax.ShapeDtypeStruct((B, S, D), F32),
        compiler_params=_cparams("arbitrary", "arbitrary"),
        name="combine",
    )(x1, yg, meta, mod6)


def _tile(n, pref):
    t = pref
    while n % t:
        t //= 2
    return t


def _layer(x, c, w_ada, b_ada, norm1_g, w_in, conv_w, conv_b, w_rg_a, b_rg_a, w_rg_x, b_rg_x,
           lru_lambda, q_norm_g, k_norm_g, kidx_norm_g, rg_out_g, attn_out_g, w_out, norm2_g,
           w_router, b_router, w1, b1, w2, b2):
    B, S, D = x.shape
    T = B * S
    tm = _tile(S, 512)
    mod6 = _ada(c, w_ada, b_ada).reshape(B, 6, D)

    xr, xg, q, k, v, qi, kiw = _inproj(x, mod6, norm1_g, w_in, q_norm_g, k_norm_g, kidx_norm_g, tm)
    y_rnn = _rglru(xr, xg, conv_w, conv_b, w_rg_a, b_rg_a, w_rg_x, b_rg_x, lru_lambda, rg_out_g,
                   _tile(S, 256))

    ki_ext = kiw[..., :IDX_DIM].astype(BF16)
    qi_ext = jnp.transpose(qi.reshape(B, S, IDX_HEADS, IDX_DIM), (0, 2, 3, 1))
    q_t = jnp.swapaxes(q, 1, 2)
    w_t = jnp.swapaxes(kiw[..., IDX_DIM:IDX_DIM + IDX_HEADS], 1, 2)
    v_h = v.reshape(B, S, N_HEADS, HEAD_DIM)
    v_ext = jnp.concatenate([v_h, jnp.ones((B, S, N_HEADS, 1), BF16),
                             jnp.zeros((B, S, N_HEADS, V_ROWS - HEAD_DIM - 1), BF16)], axis=-1)
    vt = jnp.swapaxes(v_ext.reshape(B, S, N_HEADS * V_ROWS), 1, 2)
    qb = _tile(S, 256)
    y_attn_t = _dsa(ki_ext, k, vt, qi_ext, q_t, w_t, qb, _tile(S, 512))
    y_attn = jnp.swapaxes(y_attn_t, 1, 2)

    x1, h2, meta, cnt = _outproj(x, y_rnn, y_attn, mod6, attn_out_g, w_out, norm2_g,
                                 w_router, b_router, tm)

    mb = 512
    top_e = meta[..., 0:TOPK_EXPERTS].astype(jnp.int32).reshape(T, TOPK_EXPERTS)
    rank = meta[..., 2 * TOPK_EXPERTS:3 * TOPK_EXPERTS].astype(jnp.int32).reshape(T, TOPK_EXPERTS)
    counts = cnt[0, :N_EXPERTS].astype(jnp.int32)
    padded = (counts + mb - 1) // mb * mb
    pad_end = jnp.cumsum(padded)
    pad_start = pad_end - padded
    dest = pad_start[top_e] + rank
    n_rows = T * TOPK_EXPERTS + N_EXPERTS * mb
    n_blocks = n_rows // mb
    block_start = jnp.arange(n_blocks, dtype=jnp.int32) * mb
    block_e = jnp.minimum(jnp.sum((pad_end[None, :] <= block_start[:, None]).astype(jnp.int32), axis=1),
                          N_EXPERTS - 1)
    tok = jnp.broadcast_to(jnp.arange(T, dtype=jnp.int32)[:, None], (T, TOPK_EXPERTS))
    _, sorted_tok = lax.sort_key_val(dest.reshape(-1), tok.reshape(-1))
    group_start = jnp.cumsum(counts) - counts
    pad_before = jnp.repeat((pad_start - group_start)[block_e], mb)
    pos = jnp.clip(jnp.arange(n_rows, dtype=jnp.int32) - pad_before, 0, T * TOPK_EXPERTS - 1)
    row_tok = sorted_tok[pos]
    x_sorted = h2.reshape(T, D // 2)[row_tok]
    n_used = (pad_end[N_EXPERTS - 1:] // mb).astype(jnp.int32)
    y_sorted = _experts(x_sorted, block_e, n_used, w1, b1, w2, b2, mb)
    yg = y_sorted[dest.T]
    return _combine(x1, yg, meta, mod6, tm)


def kernel(x, c, w_ada, b_ada, norm1_g, w_in, conv_w, conv_b, w_rg_a, b_rg_a, w_rg_x, b_rg_x,
           lru_lambda, q_norm_g, k_norm_g, kidx_norm_g, rg_out_g, attn_out_g, w_out, norm2_g,
           w_router, b_router, w1, b1, w2, b2):
    depth = w_ada.shape[0]
    for l in range(depth):
        x = _layer(x, c, w_ada[l], b_ada[l], norm1_g[l], w_in[l], conv_w[l], conv_b[l],
                   w_rg_a[l], b_rg_a[l], w_rg_x[l], b_rg_x[l], lru_lambda[l], q_norm_g[l],
                   k_norm_g[l], kidx_norm_g[l], rg_out_g[l], attn_out_g[l], w_out[l], norm2_g[l],
                   w_router[l], b_router[l], w1[l], b1[l], w2[l], b2[l])
    return x
```

```python
import functools

import jax
import jax.numpy as jnp
from jax import lax
from jax.experimental import pallas as pl
from jax.experimental.pallas import tpu as pltpu

D_MODEL = 1024
CHUNK = 64
CHUNK_SHIFT = 6
D_RNN = 512
RNN_BLOCKS = 8
RNN_BLOCK = 64
CONV_WIDTH = 4
RG_C = 8.0
N_HEADS = 8
HEAD_DIM = 64
D_ATTN = 512
IDX_HEADS = 8
IDX_DIM = 64
TOPK_KEYS_MAX = 256
N_EXPERTS = 32
TOPK_EXPERTS = 4
D_FF = 1024
SWIGLU_ALPHA = 1.702
SWIGLU_LIMIT = 7.0
EPS = 1e-6

LANES = 128
SUBLANES = 8
TINY_F32 = 1.1754943508222875e-38
REDUCE_CHAINS = 8
IDX_K = IDX_DIM
NEG_BIG = -(2.0 ** 100)
LOG2E = 1.4426950408889634
V_ROWS = 80
VMEM_LIMIT = 56 * 1024 * 1024

F32 = jnp.float32
BF16 = jnp.bfloat16


def _dot(a, b):
    return jnp.dot(a, b, preferred_element_type=F32)


def _dot_nt(a, b):
    return lax.dot_general(a, b, (((1,), (1,)), ((), ())), preferred_element_type=F32)


def _split(a):
    hi = a.astype(BF16)
    lo = (a - hi.astype(F32)).astype(BF16)
    return hi, lo


def _dot3(a_hi, a_lo, b_hi, b_lo):
    return _dot(a_hi, b_hi) + (_dot(a_hi, b_lo) + _dot(a_lo, b_hi))


def _pack_bf16_pairs(a):
    n = a.shape[1] // 2
    hi = lax.bitcast_convert_type(a[:, :n].astype(BF16).astype(F32), jnp.int32)
    lo = lax.bitcast_convert_type(a[:, n:].astype(BF16).astype(F32), jnp.int32)
    return jnp.bitwise_or(hi, lax.shift_right_logical(lo, 16))


def _unpack_bf16_pairs(u):
    hi = lax.bitcast_convert_type(jnp.bitwise_and(u, -65536), F32)
    lo = lax.bitcast_convert_type(lax.shift_left(u, 16), F32)
    return jnp.concatenate([hi, lo], axis=1)


def _rms(x, g):
    ms = jnp.mean(x * x, axis=-1, keepdims=True)
    return x * lax.rsqrt(ms + EPS) * g


def _cparams(*sem):
    return pltpu.CompilerParams(dimension_semantics=sem, vmem_limit_bytes=VMEM_LIMIT)


def _ada_kernel(c_ref, w_ref, b_ref, o_ref):
    c = c_ref[...]
    s = c * jax.nn.sigmoid(c)
    s_hi, s_lo = _split(s)
    w_hi, w_lo = _split(w_ref[...])
    o_ref[...] = _dot3(s_hi, s_lo, w_hi, w_lo) + b_ref[...]


def _ada(c, w_ada, b_ada):
    B, D = c.shape
    N = w_ada.shape[1]
    rows = 16
    c_pad = jnp.zeros((rows, D), F32).at[:B].set(c)
    tn = 1024
    out = pl.pallas_call(
        _ada_kernel,
        grid=(N // tn,),
        in_specs=[pl.BlockSpec((rows, D), lambda j: (0, 0)),
                  pl.BlockSpec((D, tn), lambda j: (0, j)),
                  pl.BlockSpec((1, tn), lambda j: (0, j))],
        out_specs=pl.BlockSpec((rows, tn), lambda j: (0, j)),
        out_shape=jax.ShapeDtypeStruct((rows, N), F32),
        compiler_params=_cparams("arbitrary"),
        name="ada",
    )(c_pad, w_ada, b_ada.reshape(1, N))
    return out[:B]


def _inproj_kernel(x_ref, mod_ref, g1_ref, wmain_ref, widx_ref, bd_ref,
                   qg_ref, kg_ref, kig_ref,
                   xr_ref, xg_ref, q_ref, k_ref, v_ref, qi_ref, kiw_ref):
    x = x_ref[0]
    mod = mod_ref[0]
    h = _rms(x, g1_ref[...]) * (1.0 + mod[1:2]) + mod[0:1]
    h_hi = h.astype(BF16)
    bd = bd_ref[...]

    def head_norm(z, g):
        sq_hi, sq_lo = _split(z * z)
        ms = _dot(sq_hi, bd) + _dot(sq_lo, bd)
        return z * lax.rsqrt(ms + EPS) * g

    xr_ref[0] = _dot(h_hi, wmain_ref[:, 0:512])
    xg_ref[0] = _dot(h_hi, wmain_ref[:, 512:1024])
    zq = _dot(h_hi, wmain_ref[:, 1024:1536])
    q_ref[0] = (head_norm(zq, qg_ref[...]) * (HEAD_DIM ** -0.5 * LOG2E)).astype(BF16)
    zk = _dot(h_hi, wmain_ref[:, 1536:2048])
    k_ref[0] = head_norm(zk, kg_ref[...]).astype(BF16)
    v_ref[0] = _dot(h_hi, wmain_ref[:, 2048:2560]).astype(BF16)

    zi = _dot(h_hi, widx_ref[...])
    qi_ref[0] = (zi[:, 0:512] * (IDX_DIM ** -0.5)).astype(BF16)
    kw = zi[:, 512:640]
    lane = lax.broadcasted_iota(jnp.int32, kw.shape, 1)
    is_ki = lane < IDX_DIM
    ms = jnp.sum(jnp.where(is_ki, kw * kw, 0.0), axis=-1, keepdims=True) * (1.0 / IDX_DIM)
    kiw_ref[0] = jnp.where(is_ki, kw * lax.rsqrt(ms + EPS) * kig_ref[...], kw * (IDX_HEADS ** -0.5))


def _inproj(x, mod6, norm1_g, w_in, q_norm_g, k_norm_g, kidx_norm_g, tm):
    B, S, D = x.shape
    wmain = w_in[:, :2560].astype(BF16)
    widx = jnp.concatenate([w_in[:, 2560:3144], jnp.zeros((D, 640 - 584), F32)], axis=1).astype(BF16)
    head = jnp.arange(D_ATTN) // HEAD_DIM
    bd = jnp.where(head[:, None] == head[None, :], 1.0 / HEAD_DIM, 0.0).astype(BF16)
    qg = jnp.tile(q_norm_g, N_HEADS).reshape(1, D_ATTN)
    kg = jnp.tile(k_norm_g, N_HEADS).reshape(1, D_ATTN)
    kig = jnp.concatenate([kidx_norm_g, jnp.zeros((LANES - IDX_DIM,), F32)]).reshape(1, LANES)
    const = lambda shape: pl.BlockSpec(shape, lambda b, s: (0,) * len(shape))
    row = lambda n: pl.BlockSpec((1, tm, n), lambda b, s: (b, s, 0))
    return pl.pallas_call(
        _inproj_kernel,
        grid=(B, S // tm),
        in_specs=[row(D), pl.BlockSpec((1, 6, D), lambda b, s: (b, 0, 0)), const((1, D)),
                  const((D, 2560)), const((D, 640)), const((D_ATTN, D_ATTN)),
                  const((1, D_ATTN)), const((1, D_ATTN)), const((1, LANES))],
        out_specs=[row(512), row(512), row(512), row(512), row(512), row(512), row(LANES)],
        out_shape=[jax.ShapeDtypeStruct((B, S, 512), F32), jax.ShapeDtypeStruct((B, S, 512), F32),
                   jax.ShapeDtypeStruct((B, S, 512), BF16), jax.ShapeDtypeStruct((B, S, 512), BF16),
                   jax.ShapeDtypeStruct((B, S, 512), BF16), jax.ShapeDtypeStruct((B, S, 512), BF16),
                   jax.ShapeDtypeStruct((B, S, LANES), F32)],
        compiler_params=_cparams("arbitrary", "arbitrary"),
        name="inproj",
    )(x, mod6, norm1_g.reshape(1, D), wmain, widx, bd, qg, kg, kig)


def _rglru_kernel(xr_ref, xg_ref, cw_ref, cb_ref, wa_hi_ref, wa_lo_ref, wx_hi_ref, wx_lo_ref,
                  ba_ref, bx_ref, lam_ref, g_ref, y_ref, ext_ref, hlast_ref, *, ts):
    @pl.when(pl.program_id(1) == 0)
    def _():
        ext_ref[0:8, :] = jnp.zeros((8, D_RNN), F32)
        hlast_ref[...] = jnp.zeros((1, D_RNN), F32)

    xr = xr_ref[0]
    ext_ref[8:8 + ts, :] = xr
    cw = cw_ref[...]
    xc = cb_ref[...] + cw[0:1] * ext_ref[5:5 + ts, :]
    for j in range(1, CONV_WIDTH):
        xc = xc + cw[j:j + 1] * ext_ref[5 + j:5 + j + ts, :]
    ext_ref[0:8, :] = xr[ts - 8:ts]

    xc_hi, xc_lo = _split(xc)
    r = jax.nn.sigmoid(_dot3(xc_hi, xc_lo, wa_hi_ref[...], wa_lo_ref[...]) + ba_ref[...])
    i = jax.nn.sigmoid(_dot3(xc_hi, xc_lo, wx_hi_ref[...], wx_lo_ref[...]) + bx_ref[...])
    lam = lam_ref[...]
    sp = jnp.maximum(-lam, 0.0) + jnp.log1p(jnp.exp(-jnp.abs(lam)))
    log_a = (-RG_C) * r * sp
    a = jnp.exp(log_a)
    mult = jnp.sqrt(-jnp.tanh(log_a) * (a * a + 1.0))
    u = mult * i * xc

    row = lax.broadcasted_iota(jnp.int32, (ts, D_RNN), 0)
    acc_a, acc_h = a, u
    d = 1
    while d < ts:
        sh_a = pltpu.roll(acc_a, d, 0)
        sh_h = pltpu.roll(acc_h, d, 0)
        ok = row >= d
        acc_h = jnp.where(ok, acc_a * sh_h + acc_h, acc_h)
        acc_a = jnp.where(ok, acc_a * sh_a, acc_a)
        d *= 2
    hseq = acc_h + acc_a * hlast_ref[...]
    hlast_ref[...] = hseq[ts - 1:ts]

    xg = xg_ref[0]
    gelu = 0.5 * xg * (1.0 + jnp.tanh(0.7978845608028654 * (xg + 0.044715 * (xg * xg * xg))))
    y_ref[0] = _rms(gelu * hseq, g_ref[...]).astype(BF16)


def _block_diag(w):
    H, n, _ = w.shape
    eye = jnp.eye(H, dtype=w.dtype)
    return (eye[:, None, :, None] * w[:, :, None, :]).reshape(H * n, H * n)


def _rglru(xr, xg, conv_w, conv_b, w_a, b_a, w_x, b_x, lam, g, ts):
    B, S, _ = xr.shape
    wa_hi, wa_lo = _split(_block_diag(w_a))
    wx_hi, wx_lo = _split(_block_diag(w_x))
    const = lambda shape: pl.BlockSpec(shape, lambda b, s: (0,) * len(shape))
    row = pl.BlockSpec((1, ts, D_RNN), lambda b, s: (b, s, 0))
    vec = lambda a: a.reshape(1, D_RNN)
    return pl.pallas_call(
        functools.partial(_rglru_kernel, ts=ts),
        grid=(B, S // ts),
        in_specs=[row, row, const((CONV_WIDTH, D_RNN)), const((1, D_RNN)),
                  const((D_RNN, D_RNN)), const((D_RNN, D_RNN)), const((D_RNN, D_RNN)), const((D_RNN, D_RNN)),
                  const((1, D_RNN)), const((1, D_RNN)), const((1, D_RNN)), const((1, D_RNN))],
        out_specs=row,
        out_shape=jax.ShapeDtypeStruct((B, S, D_RNN), BF16),
        scratch_shapes=[pltpu.VMEM((ts + 8, D_RNN), F32), pltpu.VMEM((1, D_RNN), F32)],
        compiler_params=_cparams("arbitrary", "arbitrary"),
        name="rglru",
    )(xr, xg, conv_w, vec(conv_b), wa_hi, wa_lo, wx_hi, wx_lo, vec(b_a), vec(b_x), vec(lam), vec(g))


def _dsa_kernel(ki_ref, k_ref, vt_ref, qi_ref, q_ref, w_ref, y_ref,
                sc_ref, qh_ref, acc_ref, lga_ref, *, qb, tk, n_sel, seq):
    j = pl.program_id(1)
    n_kt = ((j + 1) * qb + tk - 1) // tk
    q_idx = j * qb + lax.broadcasted_iota(jnp.int32, (1, qb), 1)
    q_chunk = jnp.right_shift(q_idx, CHUNK_SHIFT)

    def key_index(kt):
        return kt * tk + lax.broadcasted_iota(jnp.int32, (tk, 1), 0)

    w = w_ref[0]

    def fold8(a):
        return a.reshape(tk // SUBLANES, SUBLANES, qb)

    def count8(off, pred):
        rows = tk // REDUCE_CHAINS
        parts = []
        for i in range(REDUCE_CHAINS):
            blk = sc_ref[pl.ds(off + i * rows, rows), :]
            ind = jnp.where(pred(blk, i * rows), 1.0, 0.0)
            parts.append(jnp.sum(ind.reshape(rows // SUBLANES, SUBLANES, qb), axis=0))
        while len(parts) > 1:
            parts = [parts[i] + parts[i + 1] for i in range(0, len(parts), 2)]
        return parts[0]

    def score_tile(kt, carry):
        smin, smax = carry
        off = pl.multiple_of(kt * tk, tk)
        ki = ki_ref[0, pl.ds(off, tk), :]
        sc = jnp.zeros((tk, qb), F32)
        for h in range(IDX_HEADS):
            s = _dot(ki, qi_ref[0, h])
            sc = sc + w[h:h + 1] * jnp.maximum(s, 0.0)
        adm = jnp.right_shift(key_index(kt), CHUNK_SHIFT) <= q_chunk
        sc_ref[pl.ds(off, tk), :] = jnp.where(adm, sc, -jnp.inf)
        smin = jnp.minimum(smin, jnp.min(fold8(jnp.where(adm, sc, jnp.inf)), axis=0))
        smax = jnp.maximum(smax, jnp.max(fold8(jnp.where(adm, sc, -jnp.inf)), axis=0))
        return smin, smax

    smin8, smax8 = lax.fori_loop(0, n_kt, score_tile,
                                 (jnp.full((SUBLANES, qb), jnp.inf, F32),
                                  jnp.full((SUBLANES, qb), -jnp.inf, F32)))
    smin = jnp.min(smin8, axis=0, keepdims=True)
    smax = jnp.max(smax8, axis=0, keepdims=True)

    def count_ge(t):
        def body(kt, c8):
            off = pl.multiple_of(kt * tk, tk)
            return c8 + count8(off, lambda blk, r0: blk >= t)
        c8 = lax.fori_loop(0, n_kt, body, jnp.zeros((SUBLANES, qb), F32))
        return jnp.sum(c8, axis=0, keepdims=True)

    n_adm = (q_chunk + 1) * CHUNK
    take_all = n_adm <= n_sel
    nf = float(n_sel)

    def probe_step(st, t):
        lo, hi, cnt_hi, thr, exact, done_f = st
        active = done_f == 0.0
        c = count_ge(t)
        inside = jnp.logical_and(t > lo, t < hi)
        stuck = jnp.logical_and(active, jnp.logical_not(inside))
        probed = jnp.logical_and(active, inside)
        hit = jnp.logical_and(probed, c == nf)
        move_lo = jnp.logical_and(probed, c > nf)
        move_hi = jnp.logical_and(probed, c < nf)
        finished = jnp.logical_or(hit, stuck)
        return (jnp.where(move_lo, t, lo), jnp.where(move_hi, t, hi), jnp.where(move_hi, c, cnt_hi),
                jnp.where(hit, t, jnp.where(stuck, lo, thr)), jnp.where(hit, 1.0, exact),
                jnp.where(finished, 1.0, done_f))

    def midpoint(lo, hi):
        return lo + (hi - lo) * 0.5

    zeros = jnp.zeros((1, qb), F32)
    take_all_f = jnp.where(take_all, 1.0, 0.0)
    above_max = smax + jnp.maximum(jnp.abs(smax) * 1e-6, TINY_F32)
    st = (smin, above_max, zeros, jnp.full((1, qb), -jnp.inf, F32), take_all_f, take_all_f)
    for special in (0.0, TINY_F32):
        lo, hi = st[0], st[1]
        sp = jnp.full((1, qb), special, F32)
        st = probe_step(st, jnp.where(jnp.logical_and(sp > lo, sp < hi), sp, midpoint(lo, hi)))

    def bis_cond(st):
        return jnp.min(st[5]) < 1.0

    def bis_body(st):
        return probe_step(st, midpoint(st[0], st[1]))

    lo, hi, cnt_hi, thr, exact, _ = lax.while_loop(bis_cond, bis_body, st)
    need = nf - cnt_hi

    def count_tie(jcut):
        def body(kt, c8):
            off = pl.multiple_of(kt * tk, tk)

            def tied_before_cut(blk, r0):
                key = off + r0 + lax.broadcasted_iota(jnp.int32, (blk.shape[0], 1), 0)
                return jnp.logical_and(blk == thr, key <= jcut)
            return c8 + count8(off, tied_before_cut)
        c8 = lax.fori_loop(0, n_kt, body, jnp.zeros((SUBLANES, qb), F32))
        return jnp.sum(c8, axis=0, keepdims=True)

    def tie_cond(st):
        jlo, jhi = st
        return jnp.max(jnp.where(jnp.logical_and(exact == 0.0, jlo < jhi), 1.0, 0.0)) > 0.0

    def tie_body(st):
        jlo, jhi = st
        jmid = jnp.right_shift(jlo + jhi, 1)
        ok = count_tie(jmid) >= need
        return jnp.where(ok, jlo, jmid + 1), jnp.where(ok, jmid, jhi)

    jlo, _ = lax.while_loop(tie_cond, tie_body,
                            (jnp.zeros((1, qb), jnp.int32), jnp.full((1, qb), seq - 1, jnp.int32)))
    jcut = jnp.where(take_all, -1, jnp.where(exact > 0.0, seq, jlo))

    acc_ref[...] = jnp.zeros((N_HEADS * V_ROWS, qb), F32)
    zero_half = jnp.zeros((HEAD_DIM, qb), BF16)
    for p in range(N_HEADS // 2):
        qh_ref[2 * p] = jnp.concatenate([q_ref[0, p * LANES:p * LANES + HEAD_DIM, :], zero_half], axis=0)
        qh_ref[2 * p + 1] = jnp.concatenate([zero_half, q_ref[0, p * LANES + HEAD_DIM:(p + 1) * LANES, :]],
                                            axis=0)

    def masked_logits(kt, dst_ref):
        off = pl.multiple_of(kt * tk, tk)
        blk = sc_ref[pl.ds(off, tk), :]
        sel = jnp.logical_or(blk > thr, jnp.logical_and(blk == thr, key_index(kt) <= jcut))
        bias = jnp.where(sel, 0.0, NEG_BIG).astype(BF16)
        for h in range(N_HEADS):
            p = h // 2
            kp = k_ref[0, pl.ds(off, tk), p * LANES:(p + 1) * LANES]
            dst_ref[h] = _dot(kp, qh_ref[h]).astype(BF16) + bias

    def softmax_pv(kt, src_ref, ms):
        off = pl.multiple_of(kt * tk, tk)
        m_new = []
        for h in range(N_HEADS):
            logits = src_ref[h]
            m_h = jnp.maximum(ms[h], jnp.max(logits, axis=0, keepdims=True).astype(F32))
            alpha = jnp.exp2(ms[h] - m_h)
            probs = jnp.exp2(logits - m_h.astype(BF16))
            vt = vt_ref[0, h * V_ROWS:(h + 1) * V_ROWS, pl.ds(off, tk)]
            rows = slice(h * V_ROWS, (h + 1) * V_ROWS)
            acc_ref[rows, :] = alpha * acc_ref[rows, :] + _dot(vt, probs)
            m_new.append(m_h)
        return tuple(m_new)

    def attn_tile(kt, ms):
        masked_logits(kt, lga_ref)
        return softmax_pv(kt, lga_ref, ms)

    m0 = tuple(jnp.full((1, qb), NEG_BIG, F32) for _ in range(N_HEADS))
    lax.fori_loop(0, n_kt, attn_tile, m0)
    for h in range(N_HEADS):
        num = acc_ref[h * V_ROWS:h * V_ROWS + HEAD_DIM, :]
        den = acc_ref[h * V_ROWS + HEAD_DIM:h * V_ROWS + HEAD_DIM + 1, :]
        y_ref[0, h * HEAD_DIM:(h + 1) * HEAD_DIM, :] = num / den


def _dsa(ki_ext, k, vt, qi_ext, q, w_t, qb, tk):
    B, S, _ = k.shape
    n_sel = min(TOPK_KEYS_MAX, S // 4)
    resident = lambda shape: pl.BlockSpec(shape, lambda b, j: (b, 0, 0), pipeline_mode=pl.Buffered(1))
    return pl.pallas_call(
        functools.partial(_dsa_kernel, qb=qb, tk=tk, n_sel=n_sel, seq=S),
        grid=(B, S // qb),
        in_specs=[resident((1, S, IDX_K)), resident((1, S, D_ATTN)), resident((1, N_HEADS * V_ROWS, S)),
                  pl.BlockSpec((1, IDX_HEADS, IDX_K, qb), lambda b, j: (b, 0, 0, j)),
                  pl.BlockSpec((1, D_ATTN, qb), lambda b, j: (b, 0, j)),
                  pl.BlockSpec((1, IDX_HEADS, qb), lambda b, j: (b, 0, j))],
        out_specs=pl.BlockSpec((1, D_ATTN, qb), lambda b, j: (b, 0, j)),
        out_shape=jax.ShapeDtypeStruct((B, D_ATTN, S), F32),
        scratch_shapes=[pltpu.VMEM((S, qb), F32), pltpu.VMEM((N_HEADS, LANES, qb), BF16),
                        pltpu.VMEM((N_HEADS * V_ROWS, qb), F32),
                        pltpu.VMEM((N_HEADS, tk, qb), BF16)],
        compiler_params=_cparams("arbitrary", "arbitrary"),
        name="dsa",
    )(ki_ext, k, vt, qi_ext, q, w_t)


def _outproj_kernel(x_ref, yr_ref, ya_ref, mod_ref, ag_ref, wor_ref, woa_ref, n2g_ref,
                    wr_hi_ref, wr_lo_ref, br_ref,
                    x1_ref, h2_ref, meta_ref, cnt_ref, base_ref, *, tm):
    @pl.when(jnp.logical_and(pl.program_id(0) == 0, pl.program_id(1) == 0))
    def _():
        base_ref[...] = jnp.zeros((1, LANES), F32)

    mod = mod_ref[0]
    ya = _rms(ya_ref[0], ag_ref[...]).astype(BF16)
    mix = _dot(yr_ref[0], wor_ref[...]) + _dot(ya, woa_ref[...])
    x1 = x_ref[0] + mod[2:3] * mix
    x1_ref[0] = x1
    h2 = _rms(x1, n2g_ref[...]) * (1.0 + mod[4:5]) + mod[3:4]
    h2_ref[0] = _pack_bf16_pairs(h2)
    h2_hi, h2_lo = _split(h2)
    logits = _dot3(h2_hi, h2_lo, wr_hi_ref[...], wr_lo_ref[...]) + br_ref[...]

    lane = lax.broadcasted_iota(jnp.int32, (tm, LANES), 1).astype(F32)
    work = logits
    vals, onehots, idxs = [], [], []
    for _ in range(TOPK_EXPERTS):
        mx = jnp.max(work, axis=-1, keepdims=True)
        idx = jnp.min(jnp.where(work == mx, lane, float(LANES)), axis=-1, keepdims=True)
        oh = lane == idx
        vals.append(mx)
        idxs.append(idx)
        onehots.append(oh)
        work = jnp.where(oh, -jnp.inf, work)
    es = [jnp.exp(v - vals[0]) for v in vals]
    denom = es[0] + es[1] + es[2] + es[3]
    assigned = jnp.zeros((tm, LANES), F32)
    for oh in onehots:
        assigned = assigned + jnp.where(oh, 1.0, 0.0)
    r_i = lax.broadcasted_iota(jnp.int32, (tm, tm), 0)
    c_i = lax.broadcasted_iota(jnp.int32, (tm, tm), 1)
    ltri = jnp.where(c_i < r_i, 1.0, 0.0).astype(BF16)
    prior = _dot(ltri, assigned.astype(BF16)) + base_ref[...]
    meta = jnp.zeros((tm, LANES), F32)
    for jj in range(TOPK_EXPERTS):
        rank = jnp.sum(jnp.where(onehots[jj], prior, 0.0), axis=-1, keepdims=True)
        meta = jnp.where(lane == jj, idxs[jj], meta)
        meta = jnp.where(lane == TOPK_EXPERTS + jj, es[jj] / denom, meta)
        meta = jnp.where(lane == 2 * TOPK_EXPERTS + jj, rank, meta)
    meta_ref[0] = meta
    base = base_ref[...] + jnp.sum(assigned, axis=0, keepdims=True)
    base_ref[...] = base
    cnt_ref[...] = base


def _outproj(x, y_rnn, y_attn, mod6, attn_out_g, w_out, norm2_g, w_router, b_router, tm):
    B, S, D = x.shape
    wor = w_out[:D_RNN].astype(BF16)
    woa = w_out[D_RNN:].astype(BF16)
    wr = jnp.zeros((D, LANES), F32).at[:, :N_EXPERTS].set(w_router)
    wr_hi, wr_lo = _split(wr)
    br = jnp.full((1, LANES), NEG_BIG, F32).at[0, :N_EXPERTS].set(b_router)
    const = lambda shape: pl.BlockSpec(shape, lambda b, s: (0,) * len(shape))
    row = lambda n: pl.BlockSpec((1, tm, n), lambda b, s: (b, s, 0))
    return pl.pallas_call(
        functools.partial(_outproj_kernel, tm=tm),
        grid=(B, S // tm),
        in_specs=[row(D), row(D_RNN), row(D_ATTN), pl.BlockSpec((1, 6, D), lambda b, s: (b, 0, 0)),
                  const((1, D_ATTN)), const((D_RNN, D)), const((D_ATTN, D)), const((1, D)),
                  const((D, LANES)), const((D, LANES)), const((1, LANES))],
        out_specs=[row(D), row(D // 2), row(LANES), const((1, LANES))],
        out_shape=[jax.ShapeDtypeStruct((B, S, D), F32), jax.ShapeDtypeStruct((B, S, D // 2), jnp.int32),
                   jax.ShapeDtypeStruct((B, S, LANES), F32), jax.ShapeDtypeStruct((1, LANES), F32)],
        scratch_shapes=[pltpu.VMEM((1, LANES), F32)],
        compiler_params=_cparams("arbitrary", "arbitrary"),
        name="outproj",
    )(x, y_rnn, y_attn, mod6, attn_out_g.reshape(1, D_ATTN), wor, woa, norm2_g.reshape(1, D),
      wr_hi, wr_lo, br)


GLU_GROUP = 2 * LANES


def _degroup_kernel(w_ref, p_ref, o_ref):
    p = p_ref[...]
    for g in range(2 * D_FF // GLU_GROUP):
        cols = slice(g * GLU_GROUP, (g + 1) * GLU_GROUP)
        o_ref[0, :, cols] = _dot(w_ref[0, :, cols].astype(BF16), p).astype(BF16)


def _degroup_w1(w1):
    E, D, F2 = w1.shape
    src = jnp.arange(GLU_GROUP)
    dst = jnp.where(src % 2 == 0, src // 2, LANES + src // 2)
    perm = (dst[:, None] == jnp.arange(GLU_GROUP)[None, :]).astype(BF16)
    tr = 512
    return pl.pallas_call(
        _degroup_kernel,
        grid=(E, D // tr),
        in_specs=[pl.BlockSpec((1, tr, F2), lambda e, r: (e, r, 0)),
                  pl.BlockSpec((GLU_GROUP, GLU_GROUP), lambda e, r: (0, 0))],
        out_specs=pl.BlockSpec((1, tr, F2), lambda e, r: (e, r, 0)),
        out_shape=jax.ShapeDtypeStruct((E, D, F2), BF16),
        compiler_params=_cparams("arbitrary", "arbitrary"),
        name="degroup_w1",
    )(w1, perm)


def _expert_kernel(be_ref, nb_ref, x_ref, w1_ref, b1_ref, w2_ref, b2_ref, y_ref):
    del be_ref

    @pl.when(pl.program_id(0) < nb_ref[0])
    def _():
        x = _unpack_bf16_pairs(x_ref[...]).astype(BF16)
        u = _dot(x, w1_ref[0]) + b1_ref[0]
        acts = []
        for g in range(2 * D_FF // GLU_GROUP):
            ug = jnp.minimum(u[:, g * GLU_GROUP:g * GLU_GROUP + LANES], SWIGLU_LIMIT)
            ul = jnp.clip(u[:, g * GLU_GROUP + LANES:(g + 1) * GLU_GROUP], -SWIGLU_LIMIT, SWIGLU_LIMIT)
            acts.append((ug * jax.nn.sigmoid(SWIGLU_ALPHA * ug) * (ul + 1.0)).astype(BF16))
        act = jnp.concatenate(acts, axis=1)
        y_ref[...] = _pack_bf16_pairs(_dot(act, w2_ref[0]) + b2_ref[0])

    @pl.when(pl.program_id(0) >= nb_ref[0])
    def _():
        y_ref[...] = jnp.zeros(y_ref.shape, y_ref.dtype)


def _experts(x_sorted, block_e, n_used, w1, b1, w2, b2, mb):
    n_rows, half = x_sorted.shape
    D = 2 * half
    E = w1.shape[0]
    w1p = _degroup_w1(w1)
    ng = 2 * D_FF // GLU_GROUP
    b1p = jnp.swapaxes(b1.reshape(E, ng, LANES, 2), 2, 3).reshape(E, 1, 2 * D_FF)
    wspec = lambda k, n: pl.BlockSpec((1, k, n), lambda i, be, nb: (be[i], 0, 0))
    grid_spec = pltpu.PrefetchScalarGridSpec(
        num_scalar_prefetch=2,
        grid=(n_rows // mb,),
        in_specs=[pl.BlockSpec((mb, half), lambda i, be, nb: (jnp.minimum(i, nb[0] - 1), 0)),
                  wspec(D, 2 * D_FF), wspec(1, 2 * D_FF), wspec(D_FF, D), wspec(1, D)],
        out_specs=pl.BlockSpec((mb, half), lambda i, be, nb: (i, 0)),
    )
    return pl.pallas_call(
        _expert_kernel,
        grid_spec=grid_spec,
        out_shape=jax.ShapeDtypeStruct((n_rows, half), jnp.int32),
        compiler_params=_cparams("arbitrary"),
        name="experts",
    )(block_e, n_used, x_sorted, w1p, b1p, w2.astype(BF16), b2.reshape(E, 1, D))


def _combine_kernel(x1_ref, yg_ref, meta_ref, mod_ref, o_ref):
    meta = meta_ref[0]
    ff = meta[:, TOPK_EXPERTS:TOPK_EXPERTS + 1] * _unpack_bf16_pairs(yg_ref[0])
    for jj in range(1, TOPK_EXPERTS):
        ff = ff + meta[:, TOPK_EXPERTS + jj:TOPK_EXPERTS + jj + 1] * _unpack_bf16_pairs(yg_ref[jj])
    o_ref[0] = x1_ref[0] + mod_ref[0][5:6] * ff


def _combine(x1, yg, meta, mod6, tm):
    B, S, D = x1.shape
    nt = S // tm
    return pl.pallas_call(
        _combine_kernel,
        grid=(B, nt),
        in_specs=[pl.BlockSpec((1, tm, D), lambda b, s: (b, s, 0)),
                  pl.BlockSpec((TOPK_EXPERTS, tm, D // 2), lambda b, s: (0, b * nt + s, 0)),
                  pl.BlockSpec((1, tm, LANES), lambda b, s: (b, s, 0)),
                  pl.BlockSpec((1, 6, D), lambda b, s: (b, 0, 0))],
        out_specs=pl.BlockSpec((1, tm, D), lambda b, s: (b, s, 0)),
        out_shape=jax.ShapeDtypeStruct((B, S, D), F32),
        compiler_params=_cparams("arbitrary", "arbitrary"),
        name="combine",
    )(x1, yg, meta, mod6)


def _tile(n, pref):
    t = pref
    while n % t:
        t //= 2
    return t


def _layer(x, c, w_ada, b_ada, norm1_g, w_in, conv_w, conv_b, w_rg_a, b_rg_a, w_rg_x, b_rg_x,
           lru_lambda, q_norm_g, k_norm_g, kidx_norm_g, rg_out_g, attn_out_g, w_out, norm2_g,
           w_router, b_router, w1, b1, w2, b2):
    B, S, D = x.shape
    T = B * S
    tm = _tile(S, 512)
    mod6 = _ada(c, w_ada, b_ada).reshape(B, 6, D)

    xr, xg, q, k, v, qi, kiw = _inproj(x, mod6, norm1_g, w_in, q_norm_g, k_norm_g, kidx_norm_g, tm)
    y_rnn = _rglru(xr, xg, conv_w, conv_b, w_rg_a, b_rg_a, w_rg_x, b_rg_x, lru_lambda, rg_out_g,
                   _tile(S, 256))

    ki_ext = kiw[..., :IDX_DIM].astype(BF16)
    qi_ext = jnp.transpose(qi.reshape(B, S, IDX_HEADS, IDX_DIM), (0, 2, 3, 1))
    q_t = jnp.swapaxes(q, 1, 2)
    w_t = jnp.swapaxes(kiw[..., IDX_DIM:IDX_DIM + IDX_HEADS], 1, 2)
    v_h = v.reshape(B, S, N_HEADS, HEAD_DIM)
    v_ext = jnp.concatenate([v_h, jnp.ones((B, S, N_HEADS, 1), BF16),
                             jnp.zeros((B, S, N_HEADS, V_ROWS - HEAD_DIM - 1), BF16)], axis=-1)
    vt = jnp.swapaxes(v_ext.reshape(B, S, N_HEADS * V_ROWS), 1, 2)
    qb = _tile(S, 256)
    y_attn_t = _dsa(ki_ext, k, vt, qi_ext, q_t, w_t, qb, _tile(S, 512))
    y_attn = jnp.swapaxes(y_attn_t, 1, 2)

    x1, h2, meta, cnt = _outproj(x, y_rnn, y_attn, mod6, attn_out_g, w_out, norm2_g,
                                 w_router, b_router, tm)

    mb = 512
    top_e = meta[..., 0:TOPK_EXPERTS].astype(jnp.int32).reshape(T, TOPK_EXPERTS)
    rank = meta[..., 2 * TOPK_EXPERTS:3 * TOPK_EXPERTS].astype(jnp.int32).reshape(T, TOPK_EXPERTS)
    counts = cnt[0, :N_EXPERTS].astype(jnp.int32)
    padded = (counts + mb - 1) // mb * mb
    pad_end = jnp.cumsum(padded)
    pad_start = pad_end - padded
    dest = pad_start[top_e] + rank
    n_rows = T * TOPK_EXPERTS + N_EXPERTS * mb
    n_blocks = n_rows // mb
    block_start = jnp.arange(n_blocks, dtype=jnp.int32) * mb
    block_e = jnp.minimum(jnp.sum((pad_end[None, :] <= block_start[:, None]).astype(jnp.int32), axis=1),
                          N_EXPERTS - 1)
    tok = jnp.broadcast_to(jnp.arange(T, dtype=jnp.int32)[:, None], (T, TOPK_EXPERTS))
    _, sorted_tok = lax.sort_key_val(dest.reshape(-1), tok.reshape(-1))
    group_start = jnp.cumsum(counts) - counts
    pad_before = jnp.repeat((pad_start - group_start)[block_e], mb)
    pos = jnp.clip(jnp.arange(n_rows, dtype=jnp.int32) - pad_before, 0, T * TOPK_EXPERTS - 1)
    row_tok = sorted_tok[pos]
    x_sorted = h2.reshape(T, D // 2)[row_tok]
    n_used = (pad_end[N_EXPERTS - 1:] // mb).astype(jnp.int32)
    y_sorted = _experts(x_sorted, block_e, n_used, w1, b1, w2, b2, mb)
    yg = y_sorted[dest.T]
    return _combine(x1, yg, meta, mod6, tm)


def kernel(x, c, w_ada, b_ada, norm1_g, w_in, conv_w, conv_b, w_rg_a, b_rg_a, w_rg_x, b_rg_x,
           lru_lambda, q_norm_g, k_norm_g, kidx_norm_g, rg_out_g, attn_out_g, w_out, norm2_g,
           w_router, b_router, w1, b1, w2, b2):
    depth = w_ada.shape[0]
    for l in range(depth):
        x = _layer(x, c, w_ada[l], b_ada[l], norm1_g[l], w_in[l], conv_w[l], conv_b[l],
                   w_rg_a[l], b_rg_a[l], w_rg_x[l], b_rg_x[l], lru_lambda[l], q_norm_g[l],
                   k_norm_g[l], kidx_norm_g[l], rg_out_g[l], attn_out_g[l], w_out[l], norm2_g[l],
                   w_router[l], b_router[l], w1[l], b1[l], w2[l], b2[l])
    return x
```
